```python
import math
import jax, jax.numpy as jnp
from jax import lax
import numpy as np

D_MODEL = 2048
BATCH = 8
SEQ = 2048
DEPTH = 1

CHUNK = 64
RNN_WIDTH = 1024
N_RNN_HEADS = 16
RNN_HEAD_DIM = RNN_WIDTH // N_RNN_HEADS
RNN_CONV_W = 4
C_RG = 8.0
N_SB_HEADS = 8
SB_HEAD_DIM = 128
SB_WIDTH = N_SB_HEADS * SB_HEAD_DIM
Q_BLOCK = 128
D_FF = 3 * D_MODEL
FFN_CONV_W = 3
EPS = 1e-6
IN_SPLITS = (RNN_WIDTH, RNN_WIDTH, SB_WIDTH, SB_WIDTH, SB_WIDTH, D_MODEL, D_MODEL)
IN_COLS = sum(IN_SPLITS)

kernel_name = "hawk_stickbreak_parallel_hybrid"


def rms_norm(x, g):
    xf = x.astype(jnp.float32)
    y = xf * lax.rsqrt(jnp.mean(xf * xf, axis=-1, keepdims=True) + EPS)
    return (y * g.astype(jnp.float32)).astype(x.dtype)


def causal_depthwise_conv(x, w, b):
    k_w, ch = w.shape
    y = lax.conv_general_dilated(
        x, w[:, None, :].astype(x.dtype), window_strides=(1,),
        padding=[(k_w - 1, 0)], dimension_numbers=("NWC", "WIO", "NWC"),
        feature_group_count=ch)
    return y + b.astype(x.dtype)


def rg_lru(xr, w_a, b_a, w_x, b_x, lam):
    bsz, seq, ch = xr.shape
    xh = xr.reshape(bsz, seq, N_RNN_HEADS, RNN_HEAD_DIM)
    r = jax.nn.sigmoid(jnp.einsum('bshi,hij->bshj', xh, w_a).reshape(bsz, seq, ch) + b_a)
    i = jax.nn.sigmoid(jnp.einsum('bshi,hij->bshj', xh, w_x).reshape(bsz, seq, ch) + b_x)
    log_a = -C_RG * r.astype(jnp.float32) * jax.nn.softplus(-lam.astype(jnp.float32))
    a = jnp.exp(log_a)
    b = jnp.sqrt(-jnp.expm1(2.0 * log_a)) * (i * xr).astype(jnp.float32)

    def combine(left, right):
        a_l, b_l = left
        a_r, b_r = right
        return a_l * a_r, a_r * b_l + b_r

    _, h = lax.associative_scan(combine, (a, b), axis=1)
    return h.astype(xr.dtype)


def stick_breaking_attention(q, k, v):
    bsz, seq, n_h, d_h = q.shape
    scale = d_h ** -0.5
    outs = []
    for blk in range(seq // Q_BLOCK):
        q0 = blk * Q_BLOCK
        kv_len = q0 + Q_BLOCK
        qb = q[:, q0:kv_len].astype(jnp.float32)
        kb = k[:, :kv_len].astype(jnp.float32)
        vb = v[:, :kv_len].astype(jnp.float32)
        z = jnp.einsum('bqhd,bkhd->bhqk', qb, kb) * scale
        t_idx = q0 + jnp.arange(Q_BLOCK)[:, None]
        s_idx = jnp.arange(kv_len)[None, :]
        mask = s_idx < t_idx
        log_keep = jnp.where(mask, jax.nn.log_sigmoid(-z), 0.0)
        between = lax.cumsum(log_keep, axis=3, reverse=True) - log_keep
        w = jnp.where(mask, jnp.exp(jax.nn.log_sigmoid(z) + between), 0.0)
        outs.append(jnp.einsum('bhqk,bkhd->bqhd', w, vb))
    return jnp.concatenate(outs, axis=1).astype(q.dtype)


def setup_inputs(seed: int = 0) -> dict:
    key = jax.random.key(seed)
    ks = jax.random.split(key, 24)
    f32 = jnp.float32

    def nrm(k, shape, fan_in):
        return jax.random.normal(k, shape, f32) * (fan_in ** -0.5)

    def gain(k, shape):
        return 1.0 + 0.02 * jax.random.normal(k, shape, f32)

    def bias(k, shape):
        return 0.02 * jax.random.normal(k, shape, f32)

    L = DEPTH
    u = jax.random.uniform(ks[12], (L, RNN_WIDTH), f32, 0.9, 0.999)
    root = u ** (1.0 / C_RG)
    lru_lambda = jnp.log(root) - jnp.log1p(-root)
    return {
        "x": jax.random.normal(ks[0], (BATCH, SEQ, D_MODEL), f32),
        "c": jax.random.normal(ks[1], (BATCH, D_MODEL), f32),
        "w_ada": nrm(ks[2], (L, D_MODEL, 6 * D_MODEL), D_MODEL),
        "b_ada": bias(ks[3], (L, 6 * D_MODEL)),
        "g_norm1": gain(ks[4], (L, D_MODEL)),
        "w_in": nrm(ks[5], (L, D_MODEL, IN_COLS), D_MODEL),
        "conv_rnn_w": nrm(ks[6], (L, RNN_CONV_W, RNN_WIDTH), RNN_CONV_W),
        "conv_rnn_b": bias(ks[7], (L, RNN_WIDTH)),
        "w_rg_a": nrm(ks[8], (L, N_RNN_HEADS, RNN_HEAD_DIM, RNN_HEAD_DIM), RNN_HEAD_DIM),
        "b_rg_a": bias(ks[9], (L, RNN_WIDTH)),
        "w_rg_x": nrm(ks[10], (L, N_RNN_HEADS, RNN_HEAD_DIM, RNN_HEAD_DIM), RNN_HEAD_DIM),
        "b_rg_x": bias(ks[11], (L, RNN_WIDTH)),
        "lru_lambda": lru_lambda,
        "g_q": gain(ks[13], (L, SB_HEAD_DIM)),
        "g_k": gain(ks[14], (L, SB_HEAD_DIM)),
        "w_proj_rnn": nrm(ks[15], (L, RNN_WIDTH, D_MODEL), RNN_WIDTH),
        "w_proj_sb": nrm(ks[16], (L, SB_WIDTH, D_MODEL), SB_WIDTH),
        "w_out": nrm(ks[17], (L, D_MODEL, D_MODEL), D_MODEL),
        "g_norm2": gain(ks[18], (L, D_MODEL)),
        "w_up": nrm(ks[19], (L, D_MODEL, 2 * D_FF), D_MODEL),
        "conv_ffn_w": nrm(ks[20], (L, FFN_CONV_W, D_FF), FFN_CONV_W),
        "conv_ffn_b": bias(ks[21], (L, D_FF)),
        "w_down": nrm(ks[22], (L, D_FF, D_MODEL), D_FF),
    }


def reference(x, c, w_ada, b_ada, g_norm1, w_in, conv_rnn_w, conv_rnn_b, w_rg_a, b_rg_a,
              w_rg_x, b_rg_x, lru_lambda, g_q, g_k, w_proj_rnn, w_proj_sb, w_out,
              g_norm2, w_up, conv_ffn_w, conv_ffn_b, w_down):
    bsz, seq, _ = x.shape
    split_idx = list(np.cumsum(IN_SPLITS)[:-1])
    for l in range(DEPTH):
        mod = jax.nn.silu(c) @ w_ada[l] + b_ada[l]
        shift1, scale1, gate1, shift2, scale2, gate2 = [m[:, None, :] for m in jnp.split(mod, 6, axis=-1)]

        h = rms_norm(x, g_norm1[l]) * (1.0 + scale1) + shift1
        p = h @ w_in[l]
        x_rnn, gate_rnn, q, k, v, gm_rnn, gm_sb = jnp.split(p, split_idx, axis=-1)

        xr = causal_depthwise_conv(x_rnn, conv_rnn_w[l], conv_rnn_b[l])
        hr = rg_lru(xr, w_rg_a[l], b_rg_a[l], w_rg_x[l], b_rg_x[l], lru_lambda[l])
        y_rnn = (jax.nn.gelu(gate_rnn) * hr) @ w_proj_rnn[l]

        qh = rms_norm(q.reshape(bsz, seq, N_SB_HEADS, SB_HEAD_DIM), g_q[l])
        kh = rms_norm(k.reshape(bsz, seq, N_SB_HEADS, SB_HEAD_DIM), g_k[l])
        vh = v.reshape(bsz, seq, N_SB_HEADS, SB_HEAD_DIM)
        o_sb = stick_breaking_attention(qh, kh, vh).reshape(bsz, seq, SB_WIDTH)
        y_sb = o_sb @ w_proj_sb[l]

        merged = jax.nn.sigmoid(gm_rnn) * y_rnn + jax.nn.sigmoid(gm_sb) * y_sb
        x = x + gate1 * (merged @ w_out[l])

        h2 = rms_norm(x, g_norm2[l]) * (1.0 + scale2) + shift2
        up_val, up_gate = jnp.split(h2 @ w_up[l], 2, axis=-1)
        up_gate = causal_depthwise_conv(up_gate, conv_ffn_w[l], conv_ffn_b[l])
        x = x + gate2 * ((jax.nn.gelu(up_gate) * up_val) @ w_down[l])
    return x
```

```python
import functools

import jax
import jax.numpy as jnp
from jax import lax
from jax.experimental import pallas as pl
from jax.experimental.pallas import tpu as pltpu

F32 = jnp.float32
BF16 = jnp.bfloat16

EPS = 1e-6
C_RG = 8.0
N_RNN_HEADS = 16
N_SB_HEADS = 8
RNN_CONV_W = 4
FFN_CONV_W = 3

LANES = 128
SUBLANES = 8
VMEM_LIMIT = 56 * 1024 * 1024

SCAN_FAN = 8
ATT_TQ = 256
ATT_TK = 128


def _params(sem, vmem=VMEM_LIMIT):
    return pltpu.CompilerParams(dimension_semantics=sem, vmem_limit_bytes=vmem)


def _ada_kernel(c_ref, w_ref, b_ref, o_ref):
    a = jax.nn.silu(c_ref[...])
    o_ref[...] = jnp.dot(a, w_ref[...], preferred_element_type=F32,
                         precision=lax.Precision.HIGHEST) + b_ref[...]


def _ada(c, w, b, tn=1024):
    bsz, d = c.shape
    n = w.shape[1]
    return pl.pallas_call(
        _ada_kernel,
        grid=(n // tn,),
        in_specs=[pl.BlockSpec((bsz, d), lambda j: (0, 0)),
                  pl.BlockSpec((d, tn), lambda j: (0, j)),
                  pl.BlockSpec((1, tn), lambda j: (0, j))],
        out_specs=pl.BlockSpec((bsz, tn), lambda j: (0, j)),
        out_shape=jax.ShapeDtypeStruct((bsz, n), F32),
        compiler_params=_params(("arbitrary",)),
        name="ada",
    )(c, w, b.reshape(1, n))


def _norm_rows(x_ref, h_ref, g, scale, shift, rows, chunk=128):
    def body(r, _):
        r0 = pl.multiple_of(r * chunk, chunk)
        x = x_ref[pl.ds(r0, chunk), :]
        ms = jnp.mean(x * x, axis=-1, keepdims=True)
        y = x * lax.rsqrt(ms + EPS) * g
        h_ref[pl.ds(r0, chunk), :] = (y * (1.0 + scale) + shift).astype(h_ref.dtype)
        return 0
    lax.fori_loop(0, rows // chunk, body, 0)


def _inproj_kernel(x_ref, mod_ref, g_ref, w_ref, o_ref, h_s):
    @pl.when(pl.program_id(1) == 0)
    def _():
        _norm_rows(x_ref, h_s, g_ref[...], mod_ref[0, 1:2, :], mod_ref[0, 0:1, :],
                   x_ref.shape[0])
    o_ref[...] = jnp.dot(h_s[...], w_ref[...], preferred_element_type=F32).astype(o_ref.dtype)


def _inproj(x2, mod3, g1, w, seq, tm=1024, tn=1024):
    t, d = x2.shape
    n = w.shape[1]
    tpb = seq // tm
    return pl.pallas_call(
        _inproj_kernel,
        grid=(t // tm, n // tn),
        in_specs=[pl.BlockSpec((tm, d), lambda i, j: (i, 0)),
                  pl.BlockSpec((1,) + mod3.shape[1:], lambda i, j: (i // tpb, 0, 0)),
                  pl.BlockSpec((1, d), lambda i, j: (0, 0)),
                  pl.BlockSpec((d, tn), lambda i, j: (0, j))],
        out_specs=pl.BlockSpec((tm, tn), lambda i, j: (i, j)),
        out_shape=jax.ShapeDtypeStruct((t, n), BF16),
        scratch_shapes=[pltpu.VMEM((tm, d), BF16)],
        compiler_params=_params(("arbitrary", "arbitrary")),
        name="inproj",
    )(x2, mod3, g1, w)


def _scan_up(a_ref, b_ref, n, p_ref, l_ref, a2_ref, b2_ref, grp):
    m = n // SCAN_FAN
    for g0 in range(0, m, grp):
        p = l = None
        for k in range(SCAN_FAN):
            ak = a_ref[pl.ds(g0 * SCAN_FAN + k, grp, stride=SCAN_FAN), :]
            bk = b_ref[pl.ds(g0 * SCAN_FAN + k, grp, stride=SCAN_FAN), :]
            if k == 0:
                p, l = ak, bk
            else:
                l = ak * l + bk
                p = ak * p
            p_ref[k, g0:g0 + grp, :] = p
            l_ref[k, g0:g0 + grp, :] = l
        a2_ref[g0:g0 + grp, :] = p
        b2_ref[g0:g0 + grp, :] = l


def _scan_down(p_ref, l_ref, hin_ref, m, out_ref, grp, inclusive):
    for g0 in range(0, m, grp):
        e = hin_ref[g0:g0 + grp, :]
        for k in range(SCAN_FAN):
            if inclusive:
                v = l_ref[k, g0:g0 + grp, :] + p_ref[k, g0:g0 + grp, :] * e
            elif k == 0:
                v = e
            else:
                v = l_ref[k - 1, g0:g0 + grp, :] + p_ref[k - 1, g0:g0 + grp, :] * e
            out_ref[pl.ds(g0 * SCAN_FAN + k, grp, stride=SCAN_FAN), :] = v


def _rglru_kernel(x_ref, gt_ref, cw_ref, cb_ref, wg_ref, ba_ref, bx_ref, lam_ref, o_ref,
                  xbuf, a_s, b_s, p1, l1, a2, b2, hin1, p2, l2, a3, b3, e3, *, chunk):
    seq = x_ref.shape[0]
    n2 = seq // SCAN_FAN
    n3 = n2 // SCAN_FAN

    xbuf[0:SUBLANES, :] = jnp.zeros((SUBLANES, LANES), F32)
    for r0 in range(0, seq, chunk):
        xbuf[SUBLANES + r0:SUBLANES + r0 + chunk, :] = x_ref[r0:r0 + chunk, :].astype(F32)

    lam = lam_ref[...]
    neg = -lam
    softplus_neg_lam = jnp.maximum(neg, 0.0) + jnp.log1p(jnp.exp(-jnp.abs(neg)))
    coef = -C_RG * softplus_neg_lam

    for r0 in range(0, seq, chunk):
        xr = cb_ref[...]
        for j in range(RNN_CONV_W):
            off = SUBLANES + r0 - (RNN_CONV_W - 1 - j)
            xr = xr + cw_ref[j:j + 1, :] * xbuf[off:off + chunk, :]
        g = jnp.dot(xr.astype(BF16), wg_ref[0], preferred_element_type=F32)
        r = jax.nn.sigmoid(g[:, :LANES] + ba_ref[...])
        i = jax.nn.sigmoid(g[:, LANES:] + bx_ref[...])
        log_a = coef * r
        a_s[r0:r0 + chunk, :] = jnp.exp(log_a)
        th = jnp.tanh(log_a)
        b_s[r0:r0 + chunk, :] = jnp.sqrt(-2.0 * th / (1.0 - th)) * (i * xr)

    _scan_up(a_s, b_s, seq, p1, l1, a2, b2, grp=64)
    _scan_up(a2, b2, n2, p2, l2, a3, b3, grp=n3)
    h = jnp.zeros((1, LANES), F32)
    for r in range(n3):
        e3[r:r + 1, :] = h
        h = a3[r:r + 1, :] * h + b3[r:r + 1, :]
    _scan_down(p2, l2, e3, n3, hin1, grp=n3, inclusive=False)
    _scan_down(p1, l1, hin1, n2, b_s, grp=64, inclusive=True)

    for r0 in range(0, seq, chunk):
        gt = gt_ref[r0:r0 + chunk, :].astype(F32)
        o_ref[r0:r0 + chunk, :] = (jax.nn.gelu(gt) * b_s[r0:r0 + chunk, :]).astype(o_ref.dtype)


def _rglru(p, cw, cb, wg, ba, bx, lam, bsz, seq, x_col, gate_col, chunk=256):
    width = cw.shape[1]
    nblk = width // LANES
    n2 = seq // SCAN_FAN
    n3 = n2 // SCAN_FAN
    vec = lambda: pl.BlockSpec((1, LANES), lambda b, c: (0, c))
    return pl.pallas_call(
        functools.partial(_rglru_kernel, chunk=chunk),
        grid=(bsz, nblk),
        in_specs=[pl.BlockSpec((seq, LANES), lambda b, c: (b, x_col + c)),
                  pl.BlockSpec((seq, LANES), lambda b, c: (b, gate_col + c)),
                  pl.BlockSpec((RNN_CONV_W, LANES), lambda b, c: (0, c)),
                  vec(),
                  pl.BlockSpec((1, LANES, 2 * LANES), lambda b, c: (c, 0, 0)),
                  vec(), vec(), vec()],
        out_specs=pl.BlockSpec((seq, LANES), lambda b, c: (b, c)),
        out_shape=jax.ShapeDtypeStruct((bsz * seq, width), BF16),
        scratch_shapes=[pltpu.VMEM((seq + SUBLANES, LANES), F32),
                        pltpu.VMEM((seq, LANES), F32), pltpu.VMEM((seq, LANES), F32),
                        pltpu.VMEM((SCAN_FAN, n2, LANES), F32), pltpu.VMEM((SCAN_FAN, n2, LANES), F32),
                        pltpu.VMEM((n2, LANES), F32), pltpu.VMEM((n2, LANES), F32),
                        pltpu.VMEM((n2, LANES), F32),
                        pltpu.VMEM((SCAN_FAN, n3, LANES), F32), pltpu.VMEM((SCAN_FAN, n3, LANES), F32),
                        pltpu.VMEM((n3, LANES), F32), pltpu.VMEM((n3, LANES), F32),
                        pltpu.VMEM((n3, LANES), F32)],
        compiler_params=_params(("arbitrary", "arbitrary")),
        name="rglru",
    )(p, p, cw, cb, wg, ba, bx, lam)


def _attn_kernel(q_ref, k_ref, v_ref, gq_ref, gk_ref, uu_ref, o_ref, qn_s, kn_s, acc_s, run_s):
    seq, dh = q_ref.shape
    scale = dh ** -0.5

    def norm_body(r, _):
        r0 = pl.multiple_of(r * ATT_TQ, ATT_TQ)
        q = q_ref[pl.ds(r0, ATT_TQ), :].astype(F32)
        qn = q * lax.rsqrt(jnp.mean(q * q, axis=-1, keepdims=True) + EPS) * gq_ref[...]
        qn_s[pl.ds(r0, ATT_TQ), :] = (qn * scale).astype(BF16)
        k = k_ref[pl.ds(r0, ATT_TQ), :].astype(F32)
        kn = k * lax.rsqrt(jnp.mean(k * k, axis=-1, keepdims=True) + EPS) * gk_ref[...]
        kn_s[pl.ds(r0, ATT_TQ), :] = kn.astype(BF16)
        return 0
    lax.fori_loop(0, seq // ATT_TQ, norm_body, 0)

    def tile(q, k0, diag_offset):
        kblk = kn_s[pl.ds(k0, ATT_TK), :]
        vblk = v_ref[pl.ds(k0, ATT_TK), :]
        z = lax.dot_general(q, kblk, (((1,), (1,)), ((), ())), preferred_element_type=F32)
        log_keep = -(jnp.maximum(z, 0.0) + jnp.log(1.0 + jnp.exp(-jnp.abs(z))))
        if diag_offset is not None:
            row = lax.broadcasted_iota(jnp.int32, z.shape, 0)
            col = lax.broadcasted_iota(jnp.int32, z.shape, 1) + diag_offset
            mask = col < row
            log_keep = jnp.where(mask, log_keep, 0.0)
        hi = log_keep.astype(BF16)
        lo = (log_keep - hi.astype(F32)).astype(BF16)
        cs = jnp.dot(jnp.concatenate([hi, lo], axis=1), uu_ref[...], preferred_element_type=F32)
        w = jnp.exp(z + cs[:, :ATT_TK] + run_s[...])
        if diag_offset is not None:
            w = jnp.where(mask, w, 0.0)
        acc_s[...] += jnp.dot(w.astype(BF16), vblk, preferred_element_type=F32)
        run_s[...] += cs[:, ATT_TK:]

    sub = ATT_TQ // ATT_TK

    def q_body(qi, _):
        q0 = pl.multiple_of(qi * ATT_TQ, ATT_TQ)
        q = qn_s[pl.ds(q0, ATT_TQ), :]
        acc_s[...] = jnp.zeros_like(acc_s)
        run_s[...] = jnp.zeros_like(run_s)
        for kk in reversed(range(sub)):
            tile(q, pl.multiple_of(q0 + kk * ATT_TK, ATT_TK), kk * ATT_TK)

        def k_body(j, _):
            k0 = pl.multiple_of(q0 - (j + 1) * ATT_TK, ATT_TK)
            tile(q, k0, None)
            return 0
        lax.fori_loop(0, qi * sub, k_body, 0)
        o_ref[pl.ds(q0, ATT_TQ), :] = acc_s[...].astype(o_ref.dtype)
        return 0
    lax.fori_loop(0, seq // ATT_TQ, q_body, 0)


def _attn(p, gq, gk, uu, bsz, seq, n_heads, dh, q_col, k_col, v_col):
    return pl.pallas_call(
        _attn_kernel,
        grid=(bsz, n_heads),
        in_specs=[pl.BlockSpec((seq, dh), lambda b, h: (b, q_col + h)),
                  pl.BlockSpec((seq, dh), lambda b, h: (b, k_col + h)),
                  pl.BlockSpec((seq, dh), lambda b, h: (b, v_col + h)),
                  pl.BlockSpec((1, dh), lambda b, h: (0, 0)),
                  pl.BlockSpec((1, dh), lambda b, h: (0, 0)),
                  pl.BlockSpec(uu.shape, lambda b, h: (0, 0))],
        out_specs=pl.BlockSpec((seq, dh), lambda b, h: (b, h)),
        out_shape=jax.ShapeDtypeStruct((bsz * seq, n_heads * dh), BF16),
        scratch_shapes=[pltpu.VMEM((seq, dh), BF16), pltpu.VMEM((seq, dh), BF16),
                        pltpu.VMEM((ATT_TQ, dh), F32), pltpu.VMEM((ATT_TQ, ATT_TK), F32)],
        compiler_params=_params(("arbitrary", "arbitrary")),
        name="attn",
    )(p, p, p, gq, gk, uu)


def _merge_kernel(ya_ref, ob_ref, gm_ref, x_ref, mod_ref, g2_ref, wr_ref, wsb_ref, wo_ref,
                  x1_ref, h2_ref, mg_s, *, nc):
    d = x_ref.shape[1]
    for n0 in range(0, d, nc):
        yr = jnp.dot(ya_ref[...], wr_ref[:, n0:n0 + nc], preferred_element_type=F32)
        ys = jnp.dot(ob_ref[...], wsb_ref[:, n0:n0 + nc], preferred_element_type=F32)
        gr = jax.nn.sigmoid(gm_ref[:, n0:n0 + nc].astype(F32))
        gs = jax.nn.sigmoid(gm_ref[:, d + n0:d + n0 + nc].astype(F32))
        mg_s[:, n0:n0 + nc] = (gr * yr + gs * ys).astype(BF16)
    for n0 in range(0, d, nc):
        y = jnp.dot(mg_s[...], wo_ref[:, n0:n0 + nc], preferred_element_type=F32)
        x1_ref[:, n0:n0 + nc] = x_ref[:, n0:n0 + nc] + mod_ref[0, 2:3, n0:n0 + nc] * y
    _norm_rows(x1_ref, h2_ref, g2_ref[...], mod_ref[0, 4:5, :], mod_ref[0, 3:4, :],
               x1_ref.shape[0])


def _merge(ya, ob, p, x2, mod3, g2, wr, wsb, wo, seq, tm=512, nc=512):
    t, d = x2.shape
    w = ya.shape[1]
    tpb = seq // tm
    const = lambda shape: pl.BlockSpec(shape, lambda i: (0, 0), pipeline_mode=pl.Buffered(1))
    return pl.pallas_call(
        functools.partial(_merge_kernel, nc=nc),
        grid=(t // tm,),
        in_specs=[pl.BlockSpec((tm, w), lambda i: (i, 0)),
                  pl.BlockSpec((tm, w), lambda i: (i, 0)),
                  pl.BlockSpec((tm, 2 * d), lambda i: (i, 0)),
                  pl.BlockSpec((tm, d), lambda i: (i, 0)),
                  pl.BlockSpec((1,) + mod3.shape[1:], lambda i: (i // tpb, 0, 0)),
                  pl.BlockSpec((1, d), lambda i: (0, 0)),
                  const(wr.shape), const(wsb.shape), const(wo.shape)],
        out_specs=[pl.BlockSpec((tm, d), lambda i: (i, 0)),
                   pl.BlockSpec((tm, d), lambda i: (i, 0))],
        out_shape=[jax.ShapeDtypeStruct((t, d), F32), jax.ShapeDtypeStruct((t, d), BF16)],
        scratch_shapes=[pltpu.VMEM((tm, d), BF16)],
        compiler_params=_params(("arbitrary",)),
        name="merge",
    )(ya, ob, p, x2, mod3, g2, wr, wsb, wo)


def _ffn_kernel(h_ref, x1_ref, mod_ref, wv_ref, wg_ref, cw_ref, cb_ref, wd_ref, o_ref,
                ubuf, halo, *, tiles_per_seq):
    i = pl.program_id(0)
    f = pl.program_id(1)
    tm = h_ref.shape[0]
    h = h_ref[...]
    val = jnp.dot(h, wv_ref[...], preferred_element_type=F32)
    u = jnp.dot(h, wg_ref[...], preferred_element_type=F32)

    @pl.when(i % tiles_per_seq == 0)
    def _():
        ubuf[0:SUBLANES, :] = jnp.zeros((SUBLANES, ubuf.shape[1]), F32)

    @pl.when(i % tiles_per_seq != 0)
    def _():
        ubuf[0:SUBLANES, :] = halo[f]
    ubuf[SUBLANES:, :] = u
    halo[f] = u[tm - SUBLANES:, :]
    g = cb_ref[...] + cw_ref[FFN_CONV_W - 1:FFN_CONV_W, :] * u
    for j in range(FFN_CONV_W - 1):
        off = SUBLANES - (FFN_CONV_W - 1 - j)
        g = g + cw_ref[j:j + 1, :] * ubuf[off:off + tm, :]
    act = (jax.nn.gelu(g) * val).astype(BF16)
    y = jnp.dot(act, wd_ref[...], preferred_element_type=F32)

    @pl.when(f == 0)
    def _():
        o_ref[...] = y

    @pl.when(f != 0)
    def _():
        o_ref[...] += y

    @pl.when(f == pl.num_programs(1) - 1)
    def _():
        o_ref[...] = x1_ref[...] + mod_ref[0, 5:6, :] * o_ref[...]


def _ffn(h2, x1, mod3, wv, wg, cw, cb, wd, seq, tm=512, tf=1024):
    t, d = x1.shape
    dff = wv.shape[1]
    tpb = seq // tm
    nf = dff // tf
    return pl.pallas_call(
        functools.partial(_ffn_kernel, tiles_per_seq=tpb),
        grid=(t // tm, nf),
        in_specs=[pl.BlockSpec((tm, d), lambda i, f: (i, 0)),
                  pl.BlockSpec((tm, d), lambda i, f: (i, 0)),
                  pl.BlockSpec((1,) + mod3.shape[1:], lambda i, f: (i // tpb, 0, 0)),
                  pl.BlockSpec((d, tf), lambda i, f: (0, f)),
                  pl.BlockSpec((d, tf), lambda i, f: (0, f)),
                  pl.BlockSpec((FFN_CONV_W, tf), lambda i, f: (0, f)),
                  pl.BlockSpec((1, tf), lambda i, f: (0, f)),
                  pl.BlockSpec((tf, d), lambda i, f: (f, 0))],
        out_specs=pl.BlockSpec((tm, d), lambda i, f: (i, 0)),
        out_shape=jax.ShapeDtypeStruct((t, d), F32),
        scratch_shapes=[pltpu.VMEM((tm + SUBLANES, tf), F32),
                        pltpu.VMEM((nf, SUBLANES, tf), F32)],
        compiler_params=_params(("arbitrary", "arbitrary")),
        name="ffn",
    )(h2, x1, mod3, wv, wg, cw, cb, wd)


def _block_diag_pairs(w):
    nh, hd, _ = w.shape
    w = w.reshape(nh // 2, 2, hd, hd)
    z = jnp.zeros_like(w[:, 0])
    top = jnp.concatenate([w[:, 0], z], axis=2)
    bot = jnp.concatenate([z, w[:, 1]], axis=2)
    return jnp.concatenate([top, bot], axis=1)


def _cumsum_matrix():
    j = jnp.arange(2 * ATT_TK)[:, None] % ATT_TK
    s = jnp.arange(2 * ATT_TK)[None, :]
    return ((s >= ATT_TK) | (j >= s)).astype(BF16)


def kernel(x, c, w_ada, b_ada, g_norm1, w_in, conv_rnn_w, conv_rnn_b, w_rg_a, b_rg_a, w_rg_x,
           b_rg_x, lru_lambda, g_q, g_k, w_proj_rnn, w_proj_sb, w_out, g_norm2, w_up,
           conv_ffn_w, conv_ffn_b, w_down):
    bsz, seq, d = x.shape
    depth = w_ada.shape[0]
    rnn_w = conv_rnn_w.shape[2]
    dh = g_q.shape[1]
    sb_w = w_proj_sb.shape[1]
    dff = conv_ffn_w.shape[2]
    gm_w = 2 * d
    n_front = 2 * rnn_w + 3 * sb_w
    x_col = gm_w // LANES
    gate_col = x_col + rnn_w // LANES
    q_col = gate_col + rnn_w // LANES
    k_col = q_col + sb_w // dh
    v_col = k_col + sb_w // dh
    uu = _cumsum_matrix()

    x2 = x.reshape(bsz * seq, d)
    for l in range(depth):
        mod3 = _ada(c, w_ada[l], b_ada[l]).reshape(bsz, 6, d)
        w_in_p = jnp.concatenate([w_in[l][:, n_front:], w_in[l][:, :n_front]], axis=1).astype(BF16)
        p = _inproj(x2, mod3, g_norm1[l].reshape(1, d), w_in_p, seq)

        wg = jnp.concatenate([_block_diag_pairs(w_rg_a[l]), _block_diag_pairs(w_rg_x[l])],
                             axis=2).astype(BF16)
        ya = _rglru(p, conv_rnn_w[l], conv_rnn_b[l].reshape(1, rnn_w), wg,
                    b_rg_a[l].reshape(1, rnn_w), b_rg_x[l].reshape(1, rnn_w),
                    lru_lambda[l].reshape(1, rnn_w), bsz, seq, x_col, gate_col)
        ob = _attn(p, g_q[l].reshape(1, dh), g_k[l].reshape(1, dh), uu, bsz, seq,
                   sb_w // dh, dh, q_col, k_col, v_col)
        x1, h2 = _merge(ya, ob, p, x2, mod3, g_norm2[l].reshape(1, d),
                        w_proj_rnn[l].astype(BF16), w_proj_sb[l].astype(BF16),
                        w_out[l].astype(BF16), seq)
        x2 = _ffn(h2, x1, mod3, w_up[l][:, :dff].astype(BF16), w_up[l][:, dff:].astype(BF16),
                  conv_ffn_w[l], conv_ffn_b[l].reshape(1, dff), w_down[l].astype(BF16), seq)
    return x2.reshape(bsz, seq, d)
```

```python
import functools

import jax
import jax.numpy as jnp
from jax import lax
from jax.experimental import pallas as pl
from jax.experimental.pallas import tpu as pltpu

F32 = jnp.float32
BF16 = jnp.bfloat16

EPS = 1e-6
C_RG = 8.0
N_RNN_HEADS = 16
N_SB_HEADS = 8
RNN_CONV_W = 4
FFN_CONV_W = 3

LANES = 128
SUBLANES = 8
VMEM_LIMIT = 56 * 1024 * 1024

SCAN_FAN = 8
ATT_TQ = 512
ATT_TK = 128


def _params(sem, vmem=VMEM_LIMIT):
    return pltpu.CompilerParams(dimension_semantics=sem, vmem_limit_bytes=vmem)


def _ada_kernel(c_ref, w_ref, b_ref, o_ref):
    a = jax.nn.silu(c_ref[...])
    o_ref[...] = jnp.dot(a, w_ref[...], preferred_element_type=F32,
                         precision=lax.Precision.HIGHEST) + b_ref[...]


def _ada(c, w, b, tn=1024):
    bsz, d = c.shape
    n = w.shape[1]
    return pl.pallas_call(
        _ada_kernel,
        grid=(n // tn,),
        in_specs=[pl.BlockSpec((bsz, d), lambda j: (0, 0)),
                  pl.BlockSpec((d, tn), lambda j: (0, j)),
                  pl.BlockSpec((1, tn), lambda j: (0, j))],
        out_specs=pl.BlockSpec((bsz, tn), lambda j: (0, j)),
        out_shape=jax.ShapeDtypeStruct((bsz, n), F32),
        compiler_params=_params(("arbitrary",)),
        name="ada",
    )(c, w, b.reshape(1, n))


def _norm_rows(x_ref, h_ref, g, scale, shift, rows, chunk=128):
    def body(r, _):
        r0 = pl.multiple_of(r * chunk, chunk)
        x = x_ref[pl.ds(r0, chunk), :]
        ms = jnp.mean(x * x, axis=-1, keepdims=True)
        y = x * lax.rsqrt(ms + EPS) * g
        h_ref[pl.ds(r0, chunk), :] = (y * (1.0 + scale) + shift).astype(h_ref.dtype)
        return 0
    lax.fori_loop(0, rows // chunk, body, 0)


def _inproj_kernel(x_ref, mod_ref, g_ref, w_ref, o_ref, h_s):
    @pl.when(pl.program_id(1) == 0)
    def _():
        _norm_rows(x_ref, h_s, g_ref[...], mod_ref[0, 1:2, :], mod_ref[0, 0:1, :],
                   x_ref.shape[0])
    o_ref[...] = jnp.dot(h_s[...], w_ref[...], preferred_element_type=F32).astype(o_ref.dtype)


def _inproj(x2, mod3, g1, w, seq, tm=1024, tn=1024):
    t, d = x2.shape
    n = w.shape[1]
    tpb = seq // tm
    return pl.pallas_call(
        _inproj_kernel,
        grid=(t // tm, n // tn),
        in_specs=[pl.BlockSpec((tm, d), lambda i, j: (i, 0)),
                  pl.BlockSpec((1,) + mod3.shape[1:], lambda i, j: (i // tpb, 0, 0)),
                  pl.BlockSpec((1, d), lambda i, j: (0, 0)),
                  pl.BlockSpec((d, tn), lambda i, j: (0, j))],
        out_specs=pl.BlockSpec((tm, tn), lambda i, j: (i, j)),
        out_shape=jax.ShapeDtypeStruct((t, n), BF16),
        scratch_shapes=[pltpu.VMEM((tm, d), BF16)],
        compiler_params=_params(("arbitrary", "arbitrary")),
        name="inproj",
    )(x2, mod3, g1, w)


def _scan_up(a_ref, b_ref, n, p_ref, l_ref, a2_ref, b2_ref, grp):
    m = n // SCAN_FAN
    for g0 in range(0, m, grp):
        p = l = None
        for k in range(SCAN_FAN):
            ak = a_ref[pl.ds(g0 * SCAN_FAN + k, grp, stride=SCAN_FAN), :]
            bk = b_ref[pl.ds(g0 * SCAN_FAN + k, grp, stride=SCAN_FAN), :]
            if k == 0:
                p, l = ak, bk
            else:
                l = ak * l + bk
                p = ak * p
            p_ref[k, g0:g0 + grp, :] = p
            l_ref[k, g0:g0 + grp, :] = l
        a2_ref[g0:g0 + grp, :] = p
        b2_ref[g0:g0 + grp, :] = l


def _scan_down(p_ref, l_ref, hin_ref, m, out_ref, grp, inclusive):
    for g0 in range(0, m, grp):
        e = hin_ref[g0:g0 + grp, :]
        for k in range(SCAN_FAN):
            if inclusive:
                v = l_ref[k, g0:g0 + grp, :] + p_ref[k, g0:g0 + grp, :] * e
            elif k == 0:
                v = e
            else:
                v = l_ref[k - 1, g0:g0 + grp, :] + p_ref[k - 1, g0:g0 + grp, :] * e
            out_ref[pl.ds(g0 * SCAN_FAN + k, grp, stride=SCAN_FAN), :] = v


def _rglru_kernel(x_ref, gt_ref, cw_ref, cb_ref, wg_ref, ba_ref, bx_ref, lam_ref, o_ref,
                  xbuf, a_s, b_s, p1, l1, a2, b2, hin1, p2, l2, a3, b3, e3, *, chunk):
    seq = x_ref.shape[0]
    n2 = seq // SCAN_FAN
    n3 = n2 // SCAN_FAN

    xbuf[0:SUBLANES, :] = jnp.zeros((SUBLANES, LANES), F32)
    for r0 in range(0, seq, chunk):
        xbuf[SUBLANES + r0:SUBLANES + r0 + chunk, :] = x_ref[r0:r0 + chunk, :].astype(F32)

    lam = lam_ref[...]
    neg = -lam
    softplus_neg_lam = jnp.maximum(neg, 0.0) + jnp.log1p(jnp.exp(-jnp.abs(neg)))
    coef = -C_RG * softplus_neg_lam

    for r0 in range(0, seq, chunk):
        xr = cb_ref[...]
        for j in range(RNN_CONV_W):
            off = SUBLANES + r0 - (RNN_CONV_W - 1 - j)
            xr = xr + cw_ref[j:j + 1, :] * xbuf[off:off + chunk, :]
        g = jnp.dot(xr.astype(BF16), wg_ref[0], preferred_element_type=F32)
        r = jax.nn.sigmoid(g[:, :LANES] + ba_ref[...])
        i = jax.nn.sigmoid(g[:, LANES:] + bx_ref[...])
        log_a = coef * r
        a_s[r0:r0 + chunk, :] = jnp.exp(log_a)
        th = jnp.tanh(log_a)
        b_s[r0:r0 + chunk, :] = jnp.sqrt(-2.0 * th / (1.0 - th)) * (i * xr)

    _scan_up(a_s, b_s, seq, p1, l1, a2, b2, grp=64)
    _scan_up(a2, b2, n2, p2, l2, a3, b3, grp=n3)
    h = jnp.zeros((1, LANES), F32)
    for r in range(n3):
        e3[r:r + 1, :] = h
        h = a3[r:r + 1, :] * h + b3[r:r + 1, :]
    _scan_down(p2, l2, e3, n3, hin1, grp=n3, inclusive=False)
    _scan_down(p1, l1, hin1, n2, b_s, grp=64, inclusive=True)

    for r0 in range(0, seq, chunk):
        gt = gt_ref[r0:r0 + chunk, :].astype(F32)
        o_ref[r0:r0 + chunk, :] = (jax.nn.gelu(gt) * b_s[r0:r0 + chunk, :]).astype(o_ref.dtype)


def _rglru(p, cw, cb, wg, ba, bx, lam, bsz, seq, x_col, gate_col, chunk=256):
    width = cw.shape[1]
    nblk = width // LANES
    n2 = seq // SCAN_FAN
    n3 = n2 // SCAN_FAN
    vec = lambda: pl.BlockSpec((1, LANES), lambda b, c: (0, c))
    return pl.pallas_call(
        functools.partial(_rglru_kernel, chunk=chunk),
        grid=(bsz, nblk),
        in_specs=[pl.BlockSpec((seq, LANES), lambda b, c: (b, x_col + c)),
                  pl.BlockSpec((seq, LANES), lambda b, c: (b, gate_col + c)),
                  pl.BlockSpec((RNN_CONV_W, LANES), lambda b, c: (0, c)),
                  vec(),
                  pl.BlockSpec((1, LANES, 2 * LANES), lambda b, c: (c, 0, 0)),
                  vec(), vec(), vec()],
        out_specs=pl.BlockSpec((seq, LANES), lambda b, c: (b, c)),
        out_shape=jax.ShapeDtypeStruct((bsz * seq, width), BF16),
        scratch_shapes=[pltpu.VMEM((seq + SUBLANES, LANES), F32),
                        pltpu.VMEM((seq, LANES), F32), pltpu.VMEM((seq, LANES), F32),
                        pltpu.VMEM((SCAN_FAN, n2, LANES), F32), pltpu.VMEM((SCAN_FAN, n2, LANES), F32),
                        pltpu.VMEM((n2, LANES), F32), pltpu.VMEM((n2, LANES), F32),
                        pltpu.VMEM((n2, LANES), F32),
                        pltpu.VMEM((SCAN_FAN, n3, LANES), F32), pltpu.VMEM((SCAN_FAN, n3, LANES), F32),
                        pltpu.VMEM((n3, LANES), F32), pltpu.VMEM((n3, LANES), F32),
                        pltpu.VMEM((n3, LANES), F32)],
        compiler_params=_params(("arbitrary", "arbitrary")),
        name="rglru",
    )(p, p, cw, cb, wg, ba, bx, lam)


def _attn_kernel(q_ref, k_ref, v_ref, gq_ref, gk_ref, uu_ref, o_ref, qn_s, kn_s, acc_s, run_s):
    seq, dh = q_ref.shape
    scale = dh ** -0.5

    def norm_body(r, _):
        r0 = pl.multiple_of(r * ATT_TQ, ATT_TQ)
        q = q_ref[pl.ds(r0, ATT_TQ), :].astype(F32)
        qn = q * lax.rsqrt(jnp.mean(q * q, axis=-1, keepdims=True) + EPS) * gq_ref[...]
        qn_s[pl.ds(r0, ATT_TQ), :] = (qn * scale).astype(BF16)
        k = k_ref[pl.ds(r0, ATT_TQ), :].astype(F32)
        kn = k * lax.rsqrt(jnp.mean(k * k, axis=-1, keepdims=True) + EPS) * gk_ref[...]
        kn_s[pl.ds(r0, ATT_TQ), :] = kn.astype(BF16)
        return 0
    lax.fori_loop(0, seq // ATT_TQ, norm_body, 0)

    nsub = ATT_TQ // ATT_TK

    def block(q, k0, masked):
        kblk = kn_s[pl.ds(k0, ATT_TQ), :]
        vblk = v_ref[pl.ds(k0, ATT_TQ), :]
        z = lax.dot_general(q, kblk, (((1,), (1,)), ((), ())), preferred_element_type=F32)
        log_keep = -(jnp.maximum(z, 0.0) + jnp.log(1.0 + jnp.exp(-jnp.abs(z))))
        if masked:
            mask = (lax.broadcasted_iota(jnp.int32, z.shape, 1)
                    < lax.broadcasted_iota(jnp.int32, z.shape, 0))
            log_keep = jnp.where(mask, log_keep, 0.0)
        hi = log_keep.astype(BF16)
        lo = (log_keep - hi.astype(F32)).astype(BF16)
        run = run_s[...]
        ws = [None] * nsub
        for j in reversed(range(nsub)):
            sl = slice(j * ATT_TK, (j + 1) * ATT_TK)
            cs = jnp.dot(jnp.concatenate([hi[:, sl], lo[:, sl]], axis=1), uu_ref[...],
                         preferred_element_type=F32)
            w = jnp.exp(z[:, sl] + cs[:, :ATT_TK] + run)
            if masked:
                w = jnp.where(mask[:, sl], w, 0.0)
            ws[j] = w.astype(BF16)
            run = run + cs[:, ATT_TK:]
        run_s[...] = run
        acc_s[...] += jnp.dot(jnp.concatenate(ws, axis=1), vblk, preferred_element_type=F32)

    def q_body(qi, _):
        q0 = pl.multiple_of(qi * ATT_TQ, ATT_TQ)
        q = qn_s[pl.ds(q0, ATT_TQ), :]
        acc_s[...] = jnp.zeros_like(acc_s)
        run_s[...] = jnp.zeros_like(run_s)
        block(q, q0, True)

        def k_body(j, _):
            block(q, pl.multiple_of(q0 - (j + 1) * ATT_TQ, ATT_TQ), False)
            return 0
        lax.fori_loop(0, qi, k_body, 0)
        o_ref[pl.ds(q0, ATT_TQ), :] = acc_s[...].astype(o_ref.dtype)
        return 0
    lax.fori_loop(0, seq // ATT_TQ, q_body, 0)


def _attn(p, gq, gk, uu, bsz, seq, n_heads, dh, q_col, k_col, v_col):
    return pl.pallas_call(
        _attn_kernel,
        grid=(bsz, n_heads),
        in_specs=[pl.BlockSpec((seq, dh), lambda b, h: (b, q_col + h)),
                  pl.BlockSpec((seq, dh), lambda b, h: (b, k_col + h)),
                  pl.BlockSpec((seq, dh), lambda b, h: (b, v_col + h)),
                  pl.BlockSpec((1, dh), lambda b, h: (0, 0)),
                  pl.BlockSpec((1, dh), lambda b, h: (0, 0)),
                  pl.BlockSpec(uu.shape, lambda b, h: (0, 0))],
        out_specs=pl.BlockSpec((seq, dh), lambda b, h: (b, h)),
        out_shape=jax.ShapeDtypeStruct((bsz * seq, n_heads * dh), BF16),
        scratch_shapes=[pltpu.VMEM((seq, dh), BF16), pltpu.VMEM((seq, dh), BF16),
                        pltpu.VMEM((ATT_TQ, dh), F32), pltpu.VMEM((ATT_TQ, ATT_TK), F32)],
        compiler_params=_params(("arbitrary", "arbitrary")),
        name="attn",
    )(p, p, p, gq, gk, uu)


def _merge_kernel(ya_ref, ob_ref, gm_ref, x_ref, mod_ref, g2_ref, wr_ref, wsb_ref, wo_ref,
                  x1_ref, h2_ref, mg_s, *, nc):
    d = x_ref.shape[1]
    for n0 in range(0, d, nc):
        yr = jnp.dot(ya_ref[...], wr_ref[:, n0:n0 + nc], preferred_element_type=F32)
        ys = jnp.dot(ob_ref[...], wsb_ref[:, n0:n0 + nc], preferred_element_type=F32)
        gr = jax.nn.sigmoid(gm_ref[:, n0:n0 + nc].astype(F32))
        gs = jax.nn.sigmoid(gm_ref[:, d + n0:d + n0 + nc].astype(F32))
        mg_s[:, n0:n0 + nc] = (gr * yr + gs * ys).astype(BF16)
    for n0 in range(0, d, nc):
        y = jnp.dot(mg_s[...], wo_ref[:, n0:n0 + nc], preferred_element_type=F32)
        x1_ref[:, n0:n0 + nc] = x_ref[:, n0:n0 + nc] + mod_ref[0, 2:3, n0:n0 + nc] * y
    _norm_rows(x1_ref, h2_ref, g2_ref[...], mod_ref[0, 4:5, :], mod_ref[0, 3:4, :],
               x1_ref.shape[0])


def _merge(ya, ob, p, x2, mod3, g2, wr, wsb, wo, seq, tm=512, nc=512):
    t, d = x2.shape
    w = ya.shape[1]
    tpb = seq // tm
    const = lambda shape: pl.BlockSpec(shape, lambda i: (0, 0), pipeline_mode=pl.Buffered(1))
    return pl.pallas_call(
        functools.partial(_merge_kernel, nc=nc),
        grid=(t // tm,),
        in_specs=[pl.BlockSpec((tm, w), lambda i: (i, 0)),
                  pl.BlockSpec((tm, w), lambda i: (i, 0)),
                  pl.BlockSpec((tm, 2 * d), lambda i: (i, 0)),
                  pl.BlockSpec((tm, d), lambda i: (i, 0)),
                  pl.BlockSpec((1,) + mod3.shape[1:], lambda i: (i // tpb, 0, 0)),
                  pl.BlockSpec((1, d), lambda i: (0, 0)),
                  const(wr.shape), const(wsb.shape), const(wo.shape)],
        out_specs=[pl.BlockSpec((tm, d), lambda i: (i, 0)),
                   pl.BlockSpec((tm, d), lambda i: (i, 0))],
        out_shape=[jax.ShapeDtypeStruct((t, d), F32), jax.ShapeDtypeStruct((t, d), BF16)],
        scratch_shapes=[pltpu.VMEM((tm, d), BF16)],
        compiler_params=_params(("arbitrary",)),
        name="merge",
    )(ya, ob, p, x2, mod3, g2, wr, wsb, wo)


def _ffn_kernel(h_ref, x1_ref, mod_ref, wv_ref, wg_ref, cw_ref, cb_ref, wd_ref, o_ref,
                ubuf, halo, act_s, *, tiles_per_seq, sub, nsub):
    i = pl.program_id(0)
    f = pl.program_id(1)
    tm = h_ref.shape[0]
    tf = wv_ref.shape[1]
    d = o_ref.shape[1]

    @pl.when(f == 0)
    def _():
        o_ref[...] = jnp.zeros_like(o_ref)

    @pl.when(i % tiles_per_seq == 0)
    def _():
        ubuf[0:SUBLANES, :] = jnp.zeros((SUBLANES, tf), F32)

    @pl.when(i % tiles_per_seq != 0)
    def _():
        ubuf[0:SUBLANES, :] = halo[f]

    h = h_ref[...]
    for c0 in range(0, tf, sub):
        cs = slice(c0, c0 + sub)
        val = jnp.dot(h, wv_ref[:, cs], preferred_element_type=F32)
        u = jnp.dot(h, wg_ref[:, cs], preferred_element_type=F32)
        ubuf[SUBLANES:, cs] = u
        g = cb_ref[:, cs] + cw_ref[FFN_CONV_W - 1:FFN_CONV_W, cs] * u
        for j in range(FFN_CONV_W - 1):
            off = SUBLANES - (FFN_CONV_W - 1 - j)
            g = g + cw_ref[j:j + 1, cs] * ubuf[off:off + tm, cs]
        act_s[:, cs] = (jax.nn.gelu(g) * val).astype(BF16)
    halo[f] = ubuf[tm:tm + SUBLANES, :]

    for n0 in range(0, d, nsub):
        o_ref[:, n0:n0 + nsub] += jnp.dot(act_s[...], wd_ref[:, n0:n0 + nsub],
                                          preferred_element_type=F32)

    @pl.when(f == pl.num_programs(1) - 1)
    def _():
        o_ref[...] = x1_ref[...] + mod_ref[0, 5:6, :] * o_ref[...]


def _ffn(h2, x1, mod3, w_up, cw, cb, wd, seq, tm=512, tf=1024, sub=256, nsub=512):
    t, d = x1.shape
    dff = w_up.shape[1] // 2
    tpb = seq // tm
    nf = dff // tf
    return pl.pallas_call(
        functools.partial(_ffn_kernel, tiles_per_seq=tpb, sub=sub, nsub=nsub),
        grid=(t // tm, nf),
        in_specs=[pl.BlockSpec((tm, d), lambda i, f: (i, 0)),
                  pl.BlockSpec((tm, d), lambda i, f: (i, 0)),
                  pl.BlockSpec((1,) + mod3.shape[1:], lambda i, f: (i // tpb, 0, 0)),
                  pl.BlockSpec((d, tf), lambda i, f: (0, f)),
                  pl.BlockSpec((d, tf), lambda i, f: (0, nf + f)),
                  pl.BlockSpec((FFN_CONV_W, tf), lambda i, f: (0, f)),
                  pl.BlockSpec((1, tf), lambda i, f: (0, f)),
                  pl.BlockSpec((tf, d), lambda i, f: (f, 0))],
        out_specs=pl.BlockSpec((tm, d), lambda i, f: (i, 0)),
        out_shape=jax.ShapeDtypeStruct((t, d), F32),
        scratch_shapes=[pltpu.VMEM((tm + SUBLANES, tf), F32),
                        pltpu.VMEM((nf, SUBLANES, tf), F32),
                        pltpu.VMEM((tm, tf), BF16)],
        compiler_params=_params(("arbitrary", "arbitrary")),
        name="ffn",
    )(h2, x1, mod3, w_up, w_up, cw, cb, wd)


def _block_diag_pairs(w):
    nh, hd, _ = w.shape
    w = w.reshape(nh // 2, 2, hd, hd)
    z = jnp.zeros_like(w[:, 0])
    top = jnp.concatenate([w[:, 0], z], axis=2)
    bot = jnp.concatenate([z, w[:, 1]], axis=2)
    return jnp.concatenate([top, bot], axis=1)


def _cumsum_matrix():
    j = jnp.arange(2 * ATT_TK)[:, None] % ATT_TK
    s = jnp.arange(2 * ATT_TK)[None, :]
    return ((s >= ATT_TK) | (j >= s)).astype(BF16)


def kernel(x, c, w_ada, b_ada, g_norm1, w_in, conv_rnn_w, conv_rnn_b, w_rg_a, b_rg_a, w_rg_x,
           b_rg_x, lru_lambda, g_q, g_k, w_proj_rnn, w_proj_sb, w_out, g_norm2, w_up,
           conv_ffn_w, conv_ffn_b, w_down):
    bsz, seq, d = x.shape
    depth = w_ada.shape[0]
    rnn_w = conv_rnn_w.shape[2]
    dh = g_q.shape[1]
    sb_w = w_proj_sb.shape[1]
    dff = conv_ffn_w.shape[2]
    gm_w = 2 * d
    n_front = 2 * rnn_w + 3 * sb_w
    x_col = gm_w // LANES
    gate_col = x_col + rnn_w // LANES
    q_col = gate_col + rnn_w // LANES
    k_col = q_col + sb_w // dh
    v_col = k_col + sb_w // dh
    uu = _cumsum_matrix()

    x2 = x.reshape(bsz * seq, d)
    for l in range(depth):
        mod3 = _ada(c, w_ada[l], b_ada[l]).reshape(bsz, 6, d)
        w_in_p = jnp.concatenate([w_in[l][:, n_front:], w_in[l][:, :n_front]], axis=1).astype(BF16)
        p = _inproj(x2, mod3, g_norm1[l].reshape(1, d), w_in_p, seq)

        wg = jnp.concatenate([_block_diag_pairs(w_rg_a[l]), _block_diag_pairs(w_rg_x[l])],
                             axis=2).astype(BF16)
        ya = _rglru(p, conv_rnn_w[l], conv_rnn_b[l].reshape(1, rnn_w), wg,
                    b_rg_a[l].reshape(1, rnn_w), b_rg_x[l].reshape(1, rnn_w),
                    lru_lambda[l].reshape(1, rnn_w), bsz, seq, x_col, gate_col)
        ob = _attn(p, g_q[l].reshape(1, dh), g_k[l].reshape(1, dh), uu, bsz, seq,
                   sb_w // dh, dh, q_col, k_col, v_col)
        x1, h2 = _merge(ya, ob, p, x2, mod3, g_norm2[l].reshape(1, d),
                        w_proj_rnn[l].astype(BF16), w_proj_sb[l].astype(BF16),
                        w_out[l].astype(BF16), seq)
        x2 = _ffn(h2, x1, mod3, w_up[l].astype(BF16), conv_ffn_w[l],
                  conv_ffn_b[l].reshape(1, dff), w_down[l].astype(BF16), seq)
    return x2.reshape(bsz, seq, d)
```

```python
import functools

import jax
import jax.numpy as jnp
from jax import lax
from jax.experimental import pallas as pl
from jax.experimental.pallas import tpu as pltpu

F32 = jnp.float32
BF16 = jnp.bfloat16

EPS = 1e-6
C_RG = 8.0
N_RNN_HEADS = 16
N_SB_HEADS = 8
RNN_CONV_W = 4
FFN_CONV_W = 3

LANES = 128
SUBLANES = 8
VMEM_LIMIT = 56 * 1024 * 1024

SCAN_FAN = 8
ATT_TQ = 512
ATT_TK = 128
ATT_HEADS = 2
LOG2_E = 1.4426950408889634


def _params(sem, vmem=VMEM_LIMIT, flags=None):
    return pltpu.CompilerParams(dimension_semantics=sem, vmem_limit_bytes=vmem, flags=flags)


def _ada_kernel(c_ref, w_ref, b_ref, o_ref):
    a = jax.nn.silu(c_ref[...])
    o_ref[...] = jnp.dot(a, w_ref[...], preferred_element_type=F32,
                         precision=lax.Precision.HIGHEST) + b_ref[...]


def _ada(c, w, b, tn=1024):
    bsz, d = c.shape
    n = w.shape[1]
    return pl.pallas_call(
        _ada_kernel,
        grid=(n // tn,),
        in_specs=[pl.BlockSpec((bsz, d), lambda j: (0, 0)),
                  pl.BlockSpec((d, tn), lambda j: (0, j)),
                  pl.BlockSpec((1, tn), lambda j: (0, j))],
        out_specs=pl.BlockSpec((bsz, tn), lambda j: (0, j)),
        out_shape=jax.ShapeDtypeStruct((bsz, n), F32),
        compiler_params=_params(("arbitrary",)),
        name="ada",
    )(c, w, b.reshape(1, n))


def _norm_rows(x_ref, h_ref, g, scale, shift, rows, chunk=128):
    def body(r, _):
        r0 = pl.multiple_of(r * chunk, chunk)
        x = x_ref[pl.ds(r0, chunk), :]
        ms = jnp.mean(x * x, axis=-1, keepdims=True)
        y = x * lax.rsqrt(ms + EPS) * g
        h_ref[pl.ds(r0, chunk), :] = (y * (1.0 + scale) + shift).astype(h_ref.dtype)
        return 0
    lax.fori_loop(0, rows // chunk, body, 0)


def _inproj_kernel(x_ref, mod_ref, g_ref, w_ref, o_ref, h_s):
    @pl.when(pl.program_id(1) == 0)
    def _():
        _norm_rows(x_ref, h_s, g_ref[...], mod_ref[0, 1:2, :], mod_ref[0, 0:1, :],
                   x_ref.shape[0])
    o_ref[...] = jnp.dot(h_s[...], w_ref[...], preferred_element_type=F32).astype(o_ref.dtype)


def _inproj(x2, mod3, g1, w, seq, tm=1024, tn=1536):
    t, d = x2.shape
    n = w.shape[1]
    tpb = seq // tm
    return pl.pallas_call(
        _inproj_kernel,
        grid=(t // tm, n // tn),
        in_specs=[pl.BlockSpec((tm, d), lambda i, j: (i, 0)),
                  pl.BlockSpec((1,) + mod3.shape[1:], lambda i, j: (i // tpb, 0, 0)),
                  pl.BlockSpec((1, d), lambda i, j: (0, 0)),
                  pl.BlockSpec((d, tn), lambda i, j: (0, j))],
        out_specs=pl.BlockSpec((tm, tn), lambda i, j: (i, j)),
        out_shape=jax.ShapeDtypeStruct((t, n), BF16),
        scratch_shapes=[pltpu.VMEM((tm, d), BF16)],
        compiler_params=_params(("arbitrary", "arbitrary")),
        name="inproj",
    )(x2, mod3, g1, w)


def _scan_up(a_ref, b_ref, n, p_ref, l_ref, a2_ref, b2_ref, grp):
    m = n // SCAN_FAN
    for g0 in range(0, m, grp):
        p = l = None
        for k in range(SCAN_FAN):
            ak = a_ref[pl.ds(g0 * SCAN_FAN + k, grp, stride=SCAN_FAN), :]
            bk = b_ref[pl.ds(g0 * SCAN_FAN + k, grp, stride=SCAN_FAN), :]
            if k == 0:
                p, l = ak, bk
            else:
                l = ak * l + bk
                p = ak * p
            p_ref[k, g0:g0 + grp, :] = p
            l_ref[k, g0:g0 + grp, :] = l
        a2_ref[g0:g0 + grp, :] = p
        b2_ref[g0:g0 + grp, :] = l


def _scan_down(p_ref, l_ref, hin_ref, m, out_ref, grp, inclusive):
    for g0 in range(0, m, grp):
        e = hin_ref[g0:g0 + grp, :]
        for k in range(SCAN_FAN):
            if inclusive:
                v = l_ref[k, g0:g0 + grp, :] + p_ref[k, g0:g0 + grp, :] * e
            elif k == 0:
                v = e
            else:
                v = l_ref[k - 1, g0:g0 + grp, :] + p_ref[k - 1, g0:g0 + grp, :] * e
            out_ref[pl.ds(g0 * SCAN_FAN + k, grp, stride=SCAN_FAN), :] = v


def _rglru_kernel(x_ref, gt_ref, cw_ref, cb_ref, wg_ref, ba_ref, bx_ref, lam_ref, o_ref,
                  xbuf, a_s, b_s, p1, l1, a2, b2, hin1, p2, l2, a3, b3, e3, *, chunk):
    seq = x_ref.shape[0]
    n2 = seq // SCAN_FAN
    n3 = n2 // SCAN_FAN

    xbuf[0:SUBLANES, :] = jnp.zeros((SUBLANES, LANES), F32)
    for r0 in range(0, seq, chunk):
        xbuf[SUBLANES + r0:SUBLANES + r0 + chunk, :] = x_ref[r0:r0 + chunk, :].astype(F32)

    lam = lam_ref[...]
    neg = -lam
    softplus_neg_lam = jnp.maximum(neg, 0.0) + jnp.log1p(jnp.exp(-jnp.abs(neg)))
    coef = -C_RG * softplus_neg_lam

    for r0 in range(0, seq, chunk):
        xr = cb_ref[...]
        for j in range(RNN_CONV_W):
            off = SUBLANES + r0 - (RNN_CONV_W - 1 - j)
            xr = xr + cw_ref[j:j + 1, :] * xbuf[off:off + chunk, :]
        g = jnp.dot(xr.astype(BF16), wg_ref[0], preferred_element_type=F32)
        r = jax.nn.sigmoid(g[:, :LANES] + ba_ref[...])
        i = jax.nn.sigmoid(g[:, LANES:] + bx_ref[...])
        log_a = coef * r
        a_s[r0:r0 + chunk, :] = jnp.exp(log_a)
        th = jnp.tanh(log_a)
        b_s[r0:r0 + chunk, :] = jnp.sqrt(-2.0 * th / (1.0 - th)) * (i * xr)

    _scan_up(a_s, b_s, seq, p1, l1, a2, b2, grp=64)
    _scan_up(a2, b2, n2, p2, l2, a3, b3, grp=n3)
    h = jnp.zeros((1, LANES), F32)
    for r in range(n3):
        e3[r:r + 1, :] = h
        h = a3[r:r + 1, :] * h + b3[r:r + 1, :]
    _scan_down(p2, l2, e3, n3, hin1, grp=n3, inclusive=False)
    _scan_down(p1, l1, hin1, n2, b_s, grp=64, inclusive=True)

    for r0 in range(0, seq, chunk):
        gt = gt_ref[r0:r0 + chunk, :].astype(F32)
        o_ref[r0:r0 + chunk, :] = (jax.nn.gelu(gt) * b_s[r0:r0 + chunk, :]).astype(o_ref.dtype)


def _rglru(p, cw, cb, wg, ba, bx, lam, bsz, seq, x_col, gate_col, chunk=256):
    width = cw.shape[1]
    nblk = width // LANES
    n2 = seq // SCAN_FAN
    n3 = n2 // SCAN_FAN
    vec = lambda: pl.BlockSpec((1, LANES), lambda b, c: (0, c))
    return pl.pallas_call(
        functools.partial(_rglru_kernel, chunk=chunk),
        grid=(bsz, nblk),
        in_specs=[pl.BlockSpec((seq, LANES), lambda b, c: (b, x_col + c)),
                  pl.BlockSpec((seq, LANES), lambda b, c: (b, gate_col + c)),
                  pl.BlockSpec((RNN_CONV_W, LANES), lambda b, c: (0, c)),
                  vec(),
                  pl.BlockSpec((1, LANES, 2 * LANES), lambda b, c: (c, 0, 0)),
                  vec(), vec(), vec()],
        out_specs=pl.BlockSpec((seq, LANES), lambda b, c: (b, c)),
        out_shape=jax.ShapeDtypeStruct((bsz * seq, width), BF16),
        scratch_shapes=[pltpu.VMEM((seq + SUBLANES, LANES), F32),
                        pltpu.VMEM((seq, LANES), F32), pltpu.VMEM((seq, LANES), F32),
                        pltpu.VMEM((SCAN_FAN, n2, LANES), F32), pltpu.VMEM((SCAN_FAN, n2, LANES), F32),
                        pltpu.VMEM((n2, LANES), F32), pltpu.VMEM((n2, LANES), F32),
                        pltpu.VMEM((n2, LANES), F32),
                        pltpu.VMEM((SCAN_FAN, n3, LANES), F32), pltpu.VMEM((SCAN_FAN, n3, LANES), F32),
                        pltpu.VMEM((n3, LANES), F32), pltpu.VMEM((n3, LANES), F32),
                        pltpu.VMEM((n3, LANES), F32)],
        compiler_params=_params(("arbitrary", "arbitrary")),
        name="rglru",
    )(p, p, cw, cb, wg, ba, bx, lam)


def _attn_kernel(q_ref, k_ref, v_ref, gq_ref, gk_ref, uu_ref, o_ref, qn_s, kn_s, acc_s, run_s):
    seq = q_ref.shape[0]
    dh = gq_ref.shape[1]
    scale = dh ** -0.5 * LOG2_E

    def norm_body(r, _):
        r0 = pl.multiple_of(r * ATT_TQ, ATT_TQ)
        for h in range(ATT_HEADS):
            hs = slice(h * dh, (h + 1) * dh)
            q = q_ref[pl.ds(r0, ATT_TQ), hs].astype(F32)
            qn = q * lax.rsqrt(jnp.mean(q * q, axis=-1, keepdims=True) + EPS) * gq_ref[...]
            qn_s[h, pl.ds(r0, ATT_TQ), :] = (qn * scale).astype(BF16)
            k = k_ref[pl.ds(r0, ATT_TQ), hs].astype(F32)
            kn = k * lax.rsqrt(jnp.mean(k * k, axis=-1, keepdims=True) + EPS) * gk_ref[...]
            kn_s[h, pl.ds(r0, ATT_TQ), :] = kn.astype(BF16)
        return 0
    lax.fori_loop(0, seq // ATT_TQ, norm_body, 0)

    nsub = ATT_TQ // ATT_TK

    heads = range(ATT_HEADS)

    def block(q0, k0, masked):
        zs = [lax.dot_general(qn_s[h, pl.ds(q0, ATT_TQ), :], kn_s[h, pl.ds(k0, ATT_TQ), :],
                              (((1,), (1,)), ((), ())), preferred_element_type=F32)
              for h in heads]
        if masked:
            mask = (lax.broadcasted_iota(jnp.int32, zs[0].shape, 1)
                    < lax.broadcasted_iota(jnp.int32, zs[0].shape, 0))
        his, los = [], []
        for h in heads:
            z = zs[h]
            log_keep = -(jnp.maximum(z, 0.0) + jnp.log2(1.0 + jnp.exp2(-jnp.abs(z))))
            if masked:
                log_keep = jnp.where(mask, log_keep, 0.0)
            hi = log_keep.astype(BF16)
            his.append(hi)
            los.append((log_keep - hi.astype(F32)).astype(BF16))
        runs = [run_s[h] for h in heads]
        ws = [[None] * nsub for _ in heads]
        for j in reversed(range(nsub)):
            sl = slice(j * ATT_TK, (j + 1) * ATT_TK)
            for h in heads:
                cs = jnp.dot(jnp.concatenate([his[h][:, sl], los[h][:, sl]], axis=1), uu_ref[...],
                             preferred_element_type=F32)
                w = jnp.exp2(zs[h][:, sl] + cs[:, :ATT_TK] + runs[h])
                if masked:
                    w = jnp.where(mask[:, sl], w, 0.0)
                ws[h][j] = w.astype(BF16)
                runs[h] = runs[h] + cs[:, ATT_TK:]
        for h in heads:
            run_s[h] = runs[h]
            vblk = v_ref[pl.ds(k0, ATT_TQ), h * dh:(h + 1) * dh]
            acc_s[h] += jnp.dot(jnp.concatenate(ws[h], axis=1), vblk, preferred_element_type=F32)

    def q_body(qi, _):
        q0 = pl.multiple_of(qi * ATT_TQ, ATT_TQ)
        acc_s[...] = jnp.zeros_like(acc_s)
        run_s[...] = jnp.zeros_like(run_s)
        block(q0, q0, True)

        def k_body(j, _):
            block(q0, pl.multiple_of(q0 - (j + 1) * ATT_TQ, ATT_TQ), False)
            return 0
        lax.fori_loop(0, qi, k_body, 0)
        for h in range(ATT_HEADS):
            o_ref[pl.ds(q0, ATT_TQ), h * dh:(h + 1) * dh] = acc_s[h].astype(o_ref.dtype)
        return 0
    lax.fori_loop(0, seq // ATT_TQ, q_body, 0)


def _attn(p, gq, gk, uu, bsz, seq, n_heads, dh, q_col, k_col, v_col):
    wblk = ATT_HEADS * dh
    return pl.pallas_call(
        _attn_kernel,
        grid=(bsz, n_heads // ATT_HEADS),
        in_specs=[pl.BlockSpec((seq, wblk), lambda b, h: (b, q_col + h)),
                  pl.BlockSpec((seq, wblk), lambda b, h: (b, k_col + h)),
                  pl.BlockSpec((seq, wblk), lambda b, h: (b, v_col + h)),
                  pl.BlockSpec((1, dh), lambda b, h: (0, 0)),
                  pl.BlockSpec((1, dh), lambda b, h: (0, 0)),
                  pl.BlockSpec(uu.shape, lambda b, h: (0, 0))],
        out_specs=pl.BlockSpec((seq, wblk), lambda b, h: (b, h)),
        out_shape=jax.ShapeDtypeStruct((bsz * seq, n_heads * dh), BF16),
        scratch_shapes=[pltpu.VMEM((ATT_HEADS, seq, dh), BF16),
                        pltpu.VMEM((ATT_HEADS, seq, dh), BF16),
                        pltpu.VMEM((ATT_HEADS, ATT_TQ, dh), F32),
                        pltpu.VMEM((ATT_HEADS, ATT_TQ, ATT_TK), F32)],
        compiler_params=_params(("arbitrary", "arbitrary")),
        name="attn",
    )(p, p, p, gq, gk, uu)


def _merge_kernel(ya_ref, ob_ref, gr0_ref, gr1_ref, gs0_ref, gs1_ref, x_ref, mod_ref, g2_ref,
                  wr_ref, wsb_ref, wo_ref, x1_ref, h2_ref, mg_s, *, nc):
    d = x_ref.shape[1]
    half = d // 2
    for n0 in range(0, d, nc):
        yr = jnp.dot(ya_ref[...], wr_ref[:, n0:n0 + nc], preferred_element_type=F32)
        ys = jnp.dot(ob_ref[...], wsb_ref[:, n0:n0 + nc], preferred_element_type=F32)
        gr_ref, gs_ref = (gr0_ref, gs0_ref) if n0 < half else (gr1_ref, gs1_ref)
        m0 = n0 % half
        gr = jax.nn.sigmoid(gr_ref[:, m0:m0 + nc].astype(F32))
        gs = jax.nn.sigmoid(gs_ref[:, m0:m0 + nc].astype(F32))
        mg_s[:, n0:n0 + nc] = (gr * yr + gs * ys).astype(BF16)
    for n0 in range(0, d, nc):
        y = jnp.dot(mg_s[...], wo_ref[:, n0:n0 + nc], preferred_element_type=F32)
        x1_ref[:, n0:n0 + nc] = x_ref[:, n0:n0 + nc] + mod_ref[0, 2:3, n0:n0 + nc] * y
    _norm_rows(x1_ref, h2_ref, g2_ref[...], mod_ref[0, 4:5, :], mod_ref[0, 3:4, :],
               x1_ref.shape[0])


def _merge(ya, ob, p, x2, mod3, g2, wr, wsb, wo, seq, gm_col, tm=512, nc=512):
    t, d = x2.shape
    w = ya.shape[1]
    half = d // 2
    tpb = seq // tm
    const = lambda shape: pl.BlockSpec(shape, lambda i: (0, 0), pipeline_mode=pl.Buffered(1))
    gate = lambda k: pl.BlockSpec((tm, half), lambda i: (i, gm_col + k))
    return pl.pallas_call(
        functools.partial(_merge_kernel, nc=nc),
        grid=(t // tm,),
        in_specs=[pl.BlockSpec((tm, w), lambda i: (i, 0)),
                  pl.BlockSpec((tm, w), lambda i: (i, 0)),
                  gate(0), gate(1), gate(2), gate(3),
                  pl.BlockSpec((tm, d), lambda i: (i, 0)),
                  pl.BlockSpec((1,) + mod3.shape[1:], lambda i: (i // tpb, 0, 0)),
                  pl.BlockSpec((1, d), lambda i: (0, 0)),
                  const(wr.shape), const(wsb.shape), const(wo.shape)],
        out_specs=[pl.BlockSpec((tm, d), lambda i: (i, 0)),
                   pl.BlockSpec((tm, d), lambda i: (i, 0))],
        out_shape=[jax.ShapeDtypeStruct((t, d), F32), jax.ShapeDtypeStruct((t, d), BF16)],
        scratch_shapes=[pltpu.VMEM((tm, d), BF16)],
        compiler_params=_params(("arbitrary",)),
        name="merge",
    )(ya, ob, p, p, p, p, x2, mod3, g2, wr, wsb, wo)


def _ffn_kernel(h_ref, x1_ref, mod_ref, wv_ref, wg_ref, cw_ref, cb_ref, wd_ref, o_ref,
                ubuf, halo, act_s, *, tiles_per_seq, sub, nsub):
    i = pl.program_id(0)
    f = pl.program_id(1)
    tm = h_ref.shape[0]
    tf = wv_ref.shape[1]
    d = o_ref.shape[1]

    @pl.when(f == 0)
    def _():
        o_ref[...] = jnp.zeros_like(o_ref)

    @pl.when(i % tiles_per_seq == 0)
    def _():
        ubuf[0:SUBLANES, :] = jnp.zeros((SUBLANES, tf), F32)

    @pl.when(i % tiles_per_seq != 0)
    def _():
        ubuf[0:SUBLANES, :] = halo[f]

    h = h_ref[...]
    for c0 in range(0, tf, sub):
        cs = slice(c0, c0 + sub)
        val = jnp.dot(h, wv_ref[:, cs], preferred_element_type=F32)
        u = jnp.dot(h, wg_ref[:, cs], preferred_element_type=F32)
        ubuf[SUBLANES:, cs] = u
        g = cb_ref[:, cs] + cw_ref[FFN_CONV_W - 1:FFN_CONV_W, cs] * u
        for j in range(FFN_CONV_W - 1):
            off = SUBLANES - (FFN_CONV_W - 1 - j)
            g = g + cw_ref[j:j + 1, cs] * ubuf[off:off + tm, cs]
        act_s[:, cs] = (jax.nn.gelu(g) * val).astype(BF16)
    halo[f] = ubuf[tm:tm + SUBLANES, :]

    for n0 in range(0, d, nsub):
        o_ref[:, n0:n0 + nsub] += jnp.dot(act_s[...], wd_ref[:, n0:n0 + nsub],
                                          preferred_element_type=F32)

    @pl.when(f == pl.num_programs(1) - 1)
    def _():
        o_ref[...] = x1_ref[...] + mod_ref[0, 5:6, :] * o_ref[...]


def _ffn(h2, x1, mod3, w_up, cw, cb, wd, seq, tm=512, tf=1024, sub=256, nsub=512):
    t, d = x1.shape
    dff = w_up.shape[1] // 2
    tpb = seq // tm
    nf = dff // tf
    return pl.pallas_call(
        functools.partial(_ffn_kernel, tiles_per_seq=tpb, sub=sub, nsub=nsub),
        grid=(t // tm, nf),
        in_specs=[pl.BlockSpec((tm, d), lambda i, f: (i, 0)),
                  pl.BlockSpec((tm, d), lambda i, f: (i, 0)),
                  pl.BlockSpec((1,) + mod3.shape[1:], lambda i, f: (i // tpb, 0, 0)),
                  pl.BlockSpec((d, tf), lambda i, f: (0, f)),
                  pl.BlockSpec((d, tf), lambda i, f: (0, nf + f)),
                  pl.BlockSpec((FFN_CONV_W, tf), lambda i, f: (0, f)),
                  pl.BlockSpec((1, tf), lambda i, f: (0, f)),
                  pl.BlockSpec((tf, d), lambda i, f: (f, 0))],
        out_specs=pl.BlockSpec((tm, d), lambda i, f: (i, 0)),
        out_shape=jax.ShapeDtypeStruct((t, d), F32),
        scratch_shapes=[pltpu.VMEM((tm + SUBLANES, tf), F32),
                        pltpu.VMEM((nf, SUBLANES, tf), F32),
                        pltpu.VMEM((tm, tf), BF16)],
        compiler_params=_params(("arbitrary", "arbitrary")),
        name="ffn",
    )(h2, x1, mod3, w_up, w_up, cw, cb, wd)


def _block_diag_pairs(w):
    nh, hd, _ = w.shape
    w = w.reshape(nh // 2, 2, hd, hd)
    z = jnp.zeros_like(w[:, 0])
    top = jnp.concatenate([w[:, 0], z], axis=2)
    bot = jnp.concatenate([z, w[:, 1]], axis=2)
    return jnp.concatenate([top, bot], axis=1)


def _cumsum_matrix():
    j = jnp.arange(2 * ATT_TK)[:, None] % ATT_TK
    s = jnp.arange(2 * ATT_TK)[None, :]
    return ((s >= ATT_TK) | (j >= s)).astype(BF16)


def kernel(x, c, w_ada, b_ada, g_norm1, w_in, conv_rnn_w, conv_rnn_b, w_rg_a, b_rg_a, w_rg_x,
           b_rg_x, lru_lambda, g_q, g_k, w_proj_rnn, w_proj_sb, w_out, g_norm2, w_up,
           conv_ffn_w, conv_ffn_b, w_down):
    bsz, seq, d = x.shape
    depth = w_ada.shape[0]
    rnn_w = conv_rnn_w.shape[2]
    dh = g_q.shape[1]
    sb_w = w_proj_sb.shape[1]
    dff = conv_ffn_w.shape[2]
    gate_col = rnn_w // LANES
    q_off = 2 * rnn_w
    att_blk = ATT_HEADS * dh
    q_col = q_off // att_blk
    k_col = (q_off + sb_w) // att_blk
    v_col = (q_off + 2 * sb_w) // att_blk
    gm_col = (q_off + 3 * sb_w) // (d // 2)
    assert dh == LANES and q_off % att_blk == 0 and sb_w % att_blk == 0
    assert (q_off + 3 * sb_w) % (d // 2) == 0
    uu = _cumsum_matrix()

    x2 = x.reshape(bsz * seq, d)
    for l in range(depth):
        mod3 = _ada(c, w_ada[l], b_ada[l]).reshape(bsz, 6, d)
        p = _inproj(x2, mod3, g_norm1[l].reshape(1, d), w_in[l].astype(BF16), seq)

        wg = jnp.concatenate([_block_diag_pairs(w_rg_a[l]), _block_diag_pairs(w_rg_x[l])],
                             axis=2).astype(BF16)
        ya = _rglru(p, conv_rnn_w[l], conv_rnn_b[l].reshape(1, rnn_w), wg,
                    b_rg_a[l].reshape(1, rnn_w), b_rg_x[l].reshape(1, rnn_w),
                    lru_lambda[l].reshape(1, rnn_w), bsz, seq, 0, gate_col)
        ob = _attn(p, g_q[l].reshape(1, dh), g_k[l].reshape(1, dh), uu, bsz, seq,
                   sb_w // dh, dh, q_col, k_col, v_col)
        x1, h2 = _merge(ya, ob, p, x2, mod3, g_norm2[l].reshape(1, d),
                        w_proj_rnn[l].astype(BF16), w_proj_sb[l].astype(BF16),
                        w_out[l].astype(BF16), seq, gm_col)
        x2 = _ffn(h2, x1, mod3, w_up[l].astype(BF16), conv_ffn_w[l],
                  conv_ffn_b[l].reshape(1, dff), w_down[l].astype(BF16), seq)
    return x2.reshape(bsz, seq, d)
```

```python
import functools

import jax
import jax.numpy as jnp
from jax import lax
from jax.experimental import pallas as pl
from jax.experimental.pallas import tpu as pltpu

F32 = jnp.float32
BF16 = jnp.bfloat16

EPS = 1e-6
C_RG = 8.0
N_RNN_HEADS = 16
N_SB_HEADS = 8
RNN_CONV_W = 4
FFN_CONV_W = 3

LANES = 128
SUBLANES = 8
VMEM_LIMIT = 56 * 1024 * 1024

SCAN_FAN = 8
ATT_TQ = 512
ATT_TK = 128
ATT_HEADS = 2
LOG2_E = 1.4426950408889634


def _params(sem, vmem=VMEM_LIMIT, flags=None):
    return pltpu.CompilerParams(dimension_semantics=sem, vmem_limit_bytes=vmem, flags=flags)


def _ada_kernel(c_ref, w_ref, b_ref, o_ref):
    bsz = c_ref.shape[0]
    a = jax.nn.silu(c_ref[...])
    a_hi = a.astype(BF16)
    a_lo = (a - a_hi.astype(F32)).astype(BF16)
    w = w_ref[...]
    w_hi = w.astype(BF16)
    w_lo = (w - w_hi.astype(F32)).astype(BF16)
    r = jnp.dot(jnp.concatenate([a_hi, a_lo], axis=0), w_hi, preferred_element_type=F32)
    o_ref[...] = (r[:bsz] + r[bsz:] + jnp.dot(a_hi, w_lo, preferred_element_type=F32)
                  + b_ref[...])


def _ada(c, w, b, tn=1024):
    bsz, d = c.shape
    n = w.shape[1]
    return pl.pallas_call(
        _ada_kernel,
        grid=(n // tn,),
        in_specs=[pl.BlockSpec((bsz, d), lambda j: (0, 0)),
                  pl.BlockSpec((d, tn), lambda j: (0, j)),
                  pl.BlockSpec((1, tn), lambda j: (0, j))],
        out_specs=pl.BlockSpec((bsz, tn), lambda j: (0, j)),
        out_shape=jax.ShapeDtypeStruct((bsz, n), F32),
        compiler_params=_params(("arbitrary",)),
        name="ada",
    )(c, w, b.reshape(1, n))


def _norm_rows(x_ref, h_ref, g, scale, shift, rows, chunk=128):
    def body(r, _):
        r0 = pl.multiple_of(r * chunk, chunk)
        x = x_ref[pl.ds(r0, chunk), :]
        ms = jnp.mean(x * x, axis=-1, keepdims=True)
        y = x * lax.rsqrt(ms + EPS) * g
        h_ref[pl.ds(r0, chunk), :] = (y * (1.0 + scale) + shift).astype(h_ref.dtype)
        return 0
    lax.fori_loop(0, rows // chunk, body, 0)


def _inproj_kernel(x_ref, mod_ref, g_ref, w_ref, o_ref, h_s):
    @pl.when(pl.program_id(1) == 0)
    def _():
        _norm_rows(x_ref, h_s, g_ref[...], mod_ref[0, 1:2, :], mod_ref[0, 0:1, :],
                   x_ref.shape[0])
    o_ref[...] = jnp.dot(h_s[...], w_ref[...], preferred_element_type=F32).astype(o_ref.dtype)


def _inproj(x2, mod3, g1, w, seq, tm=1024, tn=1536):
    t, d = x2.shape
    n = w.shape[1]
    tpb = seq // tm
    return pl.pallas_call(
        _inproj_kernel,
        grid=(t // tm, n // tn),
        in_specs=[pl.BlockSpec((tm, d), lambda i, j: (i, 0)),
                  pl.BlockSpec((1,) + mod3.shape[1:], lambda i, j: (i // tpb, 0, 0)),
                  pl.BlockSpec((1, d), lambda i, j: (0, 0)),
                  pl.BlockSpec((d, tn), lambda i, j: (0, j))],
        out_specs=pl.BlockSpec((tm, tn), lambda i, j: (i, j)),
        out_shape=jax.ShapeDtypeStruct((t, n), BF16),
        scratch_shapes=[pltpu.VMEM((tm, d), BF16)],
        compiler_params=_params(("arbitrary", "arbitrary")),
        name="inproj",
    )(x2, mod3, g1, w)


def _scan_up(a_ref, b_ref, n, p_ref, l_ref, a2_ref, b2_ref, grp):
    m = n // SCAN_FAN
    for g0 in range(0, m, grp):
        p = l = None
        for k in range(SCAN_FAN):
            ak = a_ref[pl.ds(g0 * SCAN_FAN + k, grp, stride=SCAN_FAN), :]
            bk = b_ref[pl.ds(g0 * SCAN_FAN + k, grp, stride=SCAN_FAN), :]
            if k == 0:
                p, l = ak, bk
            else:
                l = ak * l + bk
                p = ak * p
            p_ref[k, g0:g0 + grp, :] = p
            l_ref[k, g0:g0 + grp, :] = l
        a2_ref[g0:g0 + grp, :] = p
        b2_ref[g0:g0 + grp, :] = l


def _scan_down(p_ref, l_ref, hin_ref, m, out_ref, grp, inclusive):
    for g0 in range(0, m, grp):
        e = hin_ref[g0:g0 + grp, :]
        for k in range(SCAN_FAN):
            if inclusive:
                v = l_ref[k, g0:g0 + grp, :] + p_ref[k, g0:g0 + grp, :] * e
            elif k == 0:
                v = e
            else:
                v = l_ref[k - 1, g0:g0 + grp, :] + p_ref[k - 1, g0:g0 + grp, :] * e
            out_ref[pl.ds(g0 * SCAN_FAN + k, grp, stride=SCAN_FAN), :] = v


def _rglru_kernel(x_ref, gt_ref, cw_ref, cb_ref, wg_ref, ba_ref, bx_ref, lam_ref, o_ref,
                  xbuf, a_s, b_s, p1, l1, a2, b2, hin1, p2, l2, a3, b3, e3, *, chunk):
    seq = x_ref.shape[0]
    n2 = seq // SCAN_FAN
    n3 = n2 // SCAN_FAN

    xbuf[0:SUBLANES, :] = jnp.zeros((SUBLANES, LANES), F32)
    for r0 in range(0, seq, chunk):
        xbuf[SUBLANES + r0:SUBLANES + r0 + chunk, :] = x_ref[r0:r0 + chunk, :].astype(F32)

    lam = lam_ref[...]
    neg = -lam
    softplus_neg_lam = jnp.maximum(neg, 0.0) + jnp.log1p(jnp.exp(-jnp.abs(neg)))
    half_coef = 0.5 * C_RG * softplus_neg_lam
    half_ba = 0.5 * ba_ref[...]
    half_bx = 0.5 * bx_ref[...]

    for r0 in range(0, seq, chunk):
        xr = cb_ref[...]
        for j in range(RNN_CONV_W):
            off = SUBLANES + r0 - (RNN_CONV_W - 1 - j)
            xr = xr + cw_ref[j:j + 1, :] * xbuf[off:off + chunk, :]
        g = jnp.dot(xr.astype(BF16), wg_ref[0], preferred_element_type=F32)
        tr = jnp.tanh(g[:, :LANES] + half_ba)
        i = 0.5 + 0.5 * jnp.tanh(g[:, LANES:] + half_bx)
        y = half_coef + half_coef * tr
        a_s[r0:r0 + chunk, :] = jnp.exp2(y * -LOG2_E)
        t = jnp.tanh(y)
        b_s[r0:r0 + chunk, :] = lax.rsqrt(0.5 + 0.5 / t) * (i * xr)

    _scan_up(a_s, b_s, seq, p1, l1, a2, b2, grp=64)
    _scan_up(a2, b2, n2, p2, l2, a3, b3, grp=n3)
    h = jnp.zeros((1, LANES), F32)
    for r in range(n3):
        e3[r:r + 1, :] = h
        h = a3[r:r + 1, :] * h + b3[r:r + 1, :]
    _scan_down(p2, l2, e3, n3, hin1, grp=n3, inclusive=False)
    _scan_down(p1, l1, hin1, n2, b_s, grp=64, inclusive=True)

    for r0 in range(0, seq, chunk):
        gt = gt_ref[r0:r0 + chunk, :].astype(F32)
        o_ref[r0:r0 + chunk, :] = (jax.nn.gelu(gt) * b_s[r0:r0 + chunk, :]).astype(o_ref.dtype)


def _rglru(p, cw, cb, wg, ba, bx, lam, bsz, seq, x_col, gate_col, chunk=256):
    width = cw.shape[1]
    nblk = width // LANES
    n2 = seq // SCAN_FAN
    n3 = n2 // SCAN_FAN
    vec = lambda: pl.BlockSpec((1, LANES), lambda b, c: (0, c))
    return pl.pallas_call(
        functools.partial(_rglru_kernel, chunk=chunk),
        grid=(bsz, nblk),
        in_specs=[pl.BlockSpec((seq, LANES), lambda b, c: (b, x_col + c)),
                  pl.BlockSpec((seq, LANES), lambda b, c: (b, gate_col + c)),
                  pl.BlockSpec((RNN_CONV_W, LANES), lambda b, c: (0, c)),
                  vec(),
                  pl.BlockSpec((1, LANES, 2 * LANES), lambda b, c: (c, 0, 0)),
                  vec(), vec(), vec()],
        out_specs=pl.BlockSpec((seq, LANES), lambda b, c: (b, c)),
        out_shape=jax.ShapeDtypeStruct((bsz * seq, width), BF16),
        scratch_shapes=[pltpu.VMEM((seq + SUBLANES, LANES), F32),
                        pltpu.VMEM((seq, LANES), F32), pltpu.VMEM((seq, LANES), F32),
                        pltpu.VMEM((SCAN_FAN, n2, LANES), F32), pltpu.VMEM((SCAN_FAN, n2, LANES), F32),
                        pltpu.VMEM((n2, LANES), F32), pltpu.VMEM((n2, LANES), F32),
                        pltpu.VMEM((n2, LANES), F32),
                        pltpu.VMEM((SCAN_FAN, n3, LANES), F32), pltpu.VMEM((SCAN_FAN, n3, LANES), F32),
                        pltpu.VMEM((n3, LANES), F32), pltpu.VMEM((n3, LANES), F32),
                        pltpu.VMEM((n3, LANES), F32)],
        compiler_params=_params(("arbitrary", "arbitrary")),
        name="rglru",
    )(p, p, cw, cb, wg, ba, bx, lam)


def _attn_kernel(q_ref, k_ref, v_ref, gq_ref, gk_ref, uu_ref, o_ref, qn_s, kn_s, acc_s, run_s):
    seq = q_ref.shape[0]
    dh = gq_ref.shape[1]
    scale = dh ** -0.5 * LOG2_E

    def norm_body(r, _):
        r0 = pl.multiple_of(r * ATT_TQ, ATT_TQ)
        for h in range(ATT_HEADS):
            hs = slice(h * dh, (h + 1) * dh)
            q = q_ref[pl.ds(r0, ATT_TQ), hs].astype(F32)
            qn = q * lax.rsqrt(jnp.mean(q * q, axis=-1, keepdims=True) + EPS) * gq_ref[...]
            qn_s[h, pl.ds(r0, ATT_TQ), :] = (qn * scale).astype(BF16)
            k = k_ref[pl.ds(r0, ATT_TQ), hs].astype(F32)
            kn = k * lax.rsqrt(jnp.mean(k * k, axis=-1, keepdims=True) + EPS) * gk_ref[...]
            kn_s[h, pl.ds(r0, ATT_TQ), :] = kn.astype(BF16)
        return 0
    lax.fori_loop(0, seq // ATT_TQ, norm_body, 0)

    nsub = ATT_TQ // ATT_TK

    heads = range(ATT_HEADS)

    def block(q0, k0, masked):
        zs = [lax.dot_general(qn_s[h, pl.ds(q0, ATT_TQ), :], kn_s[h, pl.ds(k0, ATT_TQ), :],
                              (((1,), (1,)), ((), ())), preferred_element_type=F32)
              for h in heads]
        if masked:
            tri = (lax.broadcasted_iota(jnp.int32, (ATT_TK, ATT_TK), 1)
                   < lax.broadcasted_iota(jnp.int32, (ATT_TK, ATT_TK), 0))

        def on_diag_rows(x, fill):
            top = jnp.where(tri, x[:ATT_TK], fill)
            return top if x.shape[0] == ATT_TK else jnp.concatenate([top, x[ATT_TK:]], axis=0)

        def neg_log_keep(z):
            neg_abs = pltpu.bitcast(pltpu.bitcast(z, jnp.uint32) | jnp.uint32(0x80000000), F32)
            return jnp.maximum(z, 0.0) + jnp.log2(1.0 + jnp.exp2(neg_abs))

        runs = [run_s[h] for h in heads]
        ws = [[None] * nsub for _ in heads]
        if not masked:
            parts = []
            for h in heads:
                s = neg_log_keep(zs[h])
                hi = s.astype(BF16)
                parts.append((hi, (s - hi.astype(F32)).astype(BF16)))
        for j in reversed(range(nsub)):
            sl = slice(j * ATT_TK, (j + 1) * ATT_TK)
            r0 = j * ATT_TK if masked else 0
            for h in heads:
                z = zs[h][r0:, sl]
                if masked:
                    s = on_diag_rows(neg_log_keep(z), 0.0)
                    hi = s.astype(BF16)
                    lo = (s - hi.astype(F32)).astype(BF16)
                else:
                    hi, lo = parts[h][0][:, sl], parts[h][1][:, sl]
                cs = jnp.dot(jnp.concatenate([hi, lo], axis=1), uu_ref[...],
                             preferred_element_type=F32)
                w = jnp.exp2(z + cs[:, :ATT_TK] + runs[h][r0:])
                if masked:
                    w = on_diag_rows(w, 0.0)
                w = w.astype(BF16)
                run = runs[h][r0:] + cs[:, ATT_TK:]
                if r0:
                    w = jnp.concatenate([jnp.zeros((r0, ATT_TK), BF16), w], axis=0)
                    run = jnp.concatenate([runs[h][:r0], run], axis=0)
                ws[h][j] = w
                runs[h] = run
        for h in heads:
            run_s[h] = runs[h]
            vblk = v_ref[pl.ds(k0, ATT_TQ), h * dh:(h + 1) * dh]
            acc_s[h] += jnp.dot(jnp.concatenate(ws[h], axis=1), vblk, preferred_element_type=F32)

    def q_body(qi, _):
        q0 = pl.multiple_of(qi * ATT_TQ, ATT_TQ)
        acc_s[...] = jnp.zeros_like(acc_s)
        run_s[...] = jnp.zeros_like(run_s)
        block(q0, q0, True)

        def k_body(j, _):
            block(q0, pl.multiple_of(q0 - (j + 1) * ATT_TQ, ATT_TQ), False)
            return 0
        lax.fori_loop(0, qi, k_body, 0)
        for h in range(ATT_HEADS):
            o_ref[pl.ds(q0, ATT_TQ), h * dh:(h + 1) * dh] = acc_s[h].astype(o_ref.dtype)
        return 0
    lax.fori_loop(0, seq // ATT_TQ, q_body, 0)


def _attn(p, gq, gk, uu, bsz, seq, n_heads, dh, q_col, k_col, v_col):
    wblk = ATT_HEADS * dh
    return pl.pallas_call(
        _attn_kernel,
        grid=(bsz, n_heads // ATT_HEADS),
        in_specs=[pl.BlockSpec((seq, wblk), lambda b, h: (b, q_col + h)),
                  pl.BlockSpec((seq, wblk), lambda b, h: (b, k_col + h)),
                  pl.BlockSpec((seq, wblk), lambda b, h: (b, v_col + h)),
                  pl.BlockSpec((1, dh), lambda b, h: (0, 0)),
                  pl.BlockSpec((1, dh), lambda b, h: (0, 0)),
                  pl.BlockSpec(uu.shape, lambda b, h: (0, 0))],
        out_specs=pl.BlockSpec((seq, wblk), lambda b, h: (b, h)),
        out_shape=jax.ShapeDtypeStruct((bsz * seq, n_heads * dh), BF16),
        scratch_shapes=[pltpu.VMEM((ATT_HEADS, seq, dh), BF16),
                        pltpu.VMEM((ATT_HEADS, seq, dh), BF16),
                        pltpu.VMEM((ATT_HEADS, ATT_TQ, dh), F32),
                        pltpu.VMEM((ATT_HEADS, ATT_TQ, ATT_TK), F32)],
        compiler_params=_params(("arbitrary", "arbitrary")),
        name="attn",
    )(p, p, p, gq, gk, uu)


def _merge_kernel(ya_ref, ob_ref, gr0_ref, gr1_ref, gs0_ref, gs1_ref, x_ref, mod_ref, g2_ref,
                  wr_ref, wsb_ref, wo_ref, x1_ref, h2_ref, mg_s, *, nc):
    d = x_ref.shape[1]
    half = d // 2
    for n0 in range(0, d, nc):
        yr = jnp.dot(ya_ref[...], wr_ref[:, n0:n0 + nc], preferred_element_type=F32)
        ys = jnp.dot(ob_ref[...], wsb_ref[:, n0:n0 + nc], preferred_element_type=F32)
        gr_ref, gs_ref = (gr0_ref, gs0_ref) if n0 < half else (gr1_ref, gs1_ref)
        m0 = n0 % half
        gr = jax.nn.sigmoid(gr_ref[:, m0:m0 + nc].astype(F32))
        gs = jax.nn.sigmoid(gs_ref[:, m0:m0 + nc].astype(F32))
        mg_s[:, n0:n0 + nc] = (gr * yr + gs * ys).astype(BF16)
    for n0 in range(0, d, nc):
        y = jnp.dot(mg_s[...], wo_ref[:, n0:n0 + nc], preferred_element_type=F32)
        x1_ref[:, n0:n0 + nc] = x_ref[:, n0:n0 + nc] + mod_ref[0, 2:3, n0:n0 + nc] * y
    _norm_rows(x1_ref, h2_ref, g2_ref[...], mod_ref[0, 4:5, :], mod_ref[0, 3:4, :],
               x1_ref.shape[0])


def _merge(ya, ob, p, x2, mod3, g2, wr, wsb, wo, seq, gm_col, tm=512, nc=512):
    t, d = x2.shape
    w = ya.shape[1]
    half = d // 2
    tpb = seq // tm
    const = lambda shape: pl.BlockSpec(shape, lambda i: (0, 0), pipeline_mode=pl.Buffered(1))
    gate = lambda k: pl.BlockSpec((tm, half), lambda i: (i, gm_col + k))
    return pl.pallas_call(
        functools.partial(_merge_kernel, nc=nc),
        grid=(t // tm,),
        in_specs=[pl.BlockSpec((tm, w), lambda i: (i, 0)),
                  pl.BlockSpec((tm, w), lambda i: (i, 0)),
                  gate(0), gate(1), gate(2), gate(3),
                  pl.BlockSpec((tm, d), lambda i: (i, 0)),
                  pl.BlockSpec((1,) + mod3.shape[1:], lambda i: (i // tpb, 0, 0)),
                  pl.BlockSpec((1, d), lambda i: (0, 0)),
                  const(wr.shape), const(wsb.shape), const(wo.shape)],
        out_specs=[pl.BlockSpec((tm, d), lambda i: (i, 0)),
                   pl.BlockSpec((tm, d), lambda i: (i, 0))],
        out_shape=[jax.ShapeDtypeStruct((t, d), F32), jax.ShapeDtypeStruct((t, d), BF16)],
        scratch_shapes=[pltpu.VMEM((tm, d), BF16)],
        compiler_params=_params(("arbitrary",)),
        name="merge",
    )(ya, ob, p, p, p, p, x2, mod3, g2, wr, wsb, wo)


def _ffn_kernel(h_ref, x1_ref, mod_ref, wv_ref, wg_ref, cw_ref, cb_ref, wd_ref, o_ref,
                ubuf, halo, act_s, *, tiles_per_seq, sub, nsub):
    i = pl.program_id(0)
    f = pl.program_id(1)
    tm = h_ref.shape[0]
    tf = wv_ref.shape[1]
    d = o_ref.shape[1]

    @pl.when(f == 0)
    def _():
        o_ref[...] = jnp.zeros_like(o_ref)

    @pl.when(i % tiles_per_seq == 0)
    def _():
        ubuf[0:SUBLANES, :] = jnp.zeros((SUBLANES, tf), F32)

    @pl.when(i % tiles_per_seq != 0)
    def _():
        ubuf[0:SUBLANES, :] = halo[f]

    h = h_ref[...]
    for c0 in range(0, tf, sub):
        cs = slice(c0, c0 + sub)
        val = jnp.dot(h, wv_ref[:, cs], preferred_element_type=F32)
        u = jnp.dot(h, wg_ref[:, cs], preferred_element_type=F32)
        ubuf[SUBLANES:, cs] = u
        g = cb_ref[:, cs] + cw_ref[FFN_CONV_W - 1:FFN_CONV_W, cs] * u
        for j in range(FFN_CONV_W - 1):
            off = SUBLANES - (FFN_CONV_W - 1 - j)
            g = g + cw_ref[j:j + 1, cs] * ubuf[off:off + tm, cs]
        act_s[:, cs] = (jax.nn.gelu(g) * val).astype(BF16)
    halo[f] = ubuf[tm:tm + SUBLANES, :]

    for n0 in range(0, d, nsub):
        o_ref[:, n0:n0 + nsub] += jnp.dot(act_s[...], wd_ref[:, n0:n0 + nsub],
                                          preferred_element_type=F32)

    @pl.when(f == pl.num_programs(1) - 1)
    def _():
        o_ref[...] = x1_ref[...] + mod_ref[0, 5:6, :] * o_ref[...]


def _ffn(h2, x1, mod3, w_up, cw, cb, wd, seq, tm=512, tf=1024, sub=256, nsub=512):
    t, d = x1.shape
    dff = w_up.shape[1] // 2
    tpb = seq // tm
    nf = dff // tf
    return pl.pallas_call(
        functools.partial(_ffn_kernel, tiles_per_seq=tpb, sub=sub, nsub=nsub),
        grid=(t // tm, nf),
        in_specs=[pl.BlockSpec((tm, d), lambda i, f: (i, 0)),
                  pl.BlockSpec((tm, d), lambda i, f: (i, 0)),
                  pl.BlockSpec((1,) + mod3.shape[1:], lambda i, f: (i // tpb, 0, 0)),
                  pl.BlockSpec((d, tf), lambda i, f: (0, f)),
                  pl.BlockSpec((d, tf), lambda i, f: (0, nf + f)),
                  pl.BlockSpec((FFN_CONV_W, tf), lambda i, f: (0, f)),
                  pl.BlockSpec((1, tf), lambda i, f: (0, f)),
                  pl.BlockSpec((tf, d), lambda i, f: (f, 0))],
        out_specs=pl.BlockSpec((tm, d), lambda i, f: (i, 0)),
        out_shape=jax.ShapeDtypeStruct((t, d), F32),
        scratch_shapes=[pltpu.VMEM((tm + SUBLANES, tf), F32),
                        pltpu.VMEM((nf, SUBLANES, tf), F32),
                        pltpu.VMEM((tm, tf), BF16)],
        compiler_params=_params(("arbitrary", "arbitrary")),
        name="ffn",
    )(h2, x1, mod3, w_up, w_up, cw, cb, wd)


def _block_diag_pairs(w):
    nh, hd, _ = w.shape
    w = w.reshape(nh // 2, 2, hd, hd)
    z = jnp.zeros_like(w[:, 0])
    top = jnp.concatenate([w[:, 0], z], axis=2)
    bot = jnp.concatenate([z, w[:, 1]], axis=2)
    return jnp.concatenate([top, bot], axis=1)


def _cumsum_matrix():
    j = jnp.arange(2 * ATT_TK)[:, None] % ATT_TK
    s = jnp.arange(2 * ATT_TK)[None, :]
    return -((s >= ATT_TK) | (j >= s)).astype(BF16)


def kernel(x, c, w_ada, b_ada, g_norm1, w_in, conv_rnn_w, conv_rnn_b, w_rg_a, b_rg_a, w_rg_x,
           b_rg_x, lru_lambda, g_q, g_k, w_proj_rnn, w_proj_sb, w_out, g_norm2, w_up,
           conv_ffn_w, conv_ffn_b, w_down):
    bsz, seq, d = x.shape
    depth = w_ada.shape[0]
    rnn_w = conv_rnn_w.shape[2]
    dh = g_q.shape[1]
    sb_w = w_proj_sb.shape[1]
    dff = conv_ffn_w.shape[2]
    gate_col = rnn_w // LANES
    q_off = 2 * rnn_w
    att_blk = ATT_HEADS * dh
    q_col = q_off // att_blk
    k_col = (q_off + sb_w) // att_blk
    v_col = (q_off + 2 * sb_w) // att_blk
    gm_col = (q_off + 3 * sb_w) // (d // 2)
    assert dh == LANES and q_off % att_blk == 0 and sb_w % att_blk == 0
    assert (q_off + 3 * sb_w) % (d // 2) == 0
    uu = _cumsum_matrix()

    x2 = x.reshape(bsz * seq, d)
    for l in range(depth):
        mod3 = _ada(c, w_ada[l], b_ada[l]).reshape(bsz, 6, d)
        p = _inproj(x2, mod3, g_norm1[l].reshape(1, d), w_in[l].astype(BF16), seq)

        wg = (0.5 * jnp.concatenate([_block_diag_pairs(w_rg_a[l]), _block_diag_pairs(w_rg_x[l])],
                                    axis=2)).astype(BF16)
        ya = _rglru(p, conv_rnn_w[l], conv_rnn_b[l].reshape(1, rnn_w), wg,
                    b_rg_a[l].reshape(1, rnn_w), b_rg_x[l].reshape(1, rnn_w),
                    lru_lambda[l].reshape(1, rnn_w), bsz, seq, 0, gate_col)
        ob = _attn(p, g_q[l].reshape(1, dh), g_k[l].reshape(1, dh), uu, bsz, seq,
                   sb_w // dh, dh, q_col, k_col, v_col)
        x1, h2 = _merge(ya, ob, p, x2, mod3, g_norm2[l].reshape(1, d),
                        w_proj_rnn[l].astype(BF16), w_proj_sb[l].astype(BF16),
                        w_out[l].astype(BF16), seq, gm_col)
        x2 = _ffn(h2, x1, mod3, w_up[l].astype(BF16), conv_ffn_w[l],
                  conv_ffn_b[l].reshape(1, dff), w_down[l].astype(BF16), seq)
    return x2.reshape(bsz, seq, d)
```

```python
import functools

import jax
import jax.numpy as jnp
from jax import lax
from jax.experimental import pallas as pl
from jax.experimental.pallas import tpu as pltpu

F32 = jnp.float32
BF16 = jnp.bfloat16

EPS = 1e-6
C_RG = 8.0
N_RNN_HEADS = 16
N_SB_HEADS = 8
RNN_CONV_W = 4
FFN_CONV_W = 3

LANES = 128
SUBLANES = 8
VMEM_LIMIT = 56 * 1024 * 1024

SCAN_FAN = 8
ATT_TQ = 512
ATT_TK = 256
ATT_HEADS = 2
LOG2_E = 1.4426950408889634


def _params(sem, vmem=VMEM_LIMIT, flags=None):
    return pltpu.CompilerParams(dimension_semantics=sem, vmem_limit_bytes=vmem, flags=flags)


def _ada_kernel(c_ref, w_ref, b_ref, o_ref):
    bsz = c_ref.shape[0]
    a = jax.nn.silu(c_ref[...])
    a_hi = a.astype(BF16)
    a_lo = (a - a_hi.astype(F32)).astype(BF16)
    w = w_ref[...]
    w_hi = w.astype(BF16)
    w_lo = (w - w_hi.astype(F32)).astype(BF16)
    r = jnp.dot(jnp.concatenate([a_hi, a_lo], axis=0), w_hi, preferred_element_type=F32)
    o_ref[...] = (r[:bsz] + r[bsz:] + jnp.dot(a_hi, w_lo, preferred_element_type=F32)
                  + b_ref[...])


def _ada(c, w, b, tn=1024):
    bsz, d = c.shape
    n = w.shape[1]
    return pl.pallas_call(
        _ada_kernel,
        grid=(n // tn,),
        in_specs=[pl.BlockSpec((bsz, d), lambda j: (0, 0)),
                  pl.BlockSpec((d, tn), lambda j: (0, j)),
                  pl.BlockSpec((1, tn), lambda j: (0, j))],
        out_specs=pl.BlockSpec((bsz, tn), lambda j: (0, j)),
        out_shape=jax.ShapeDtypeStruct((bsz, n), F32),
        compiler_params=_params(("arbitrary",)),
        name="ada",
    )(c, w, b.reshape(1, n))


def _norm_rows(x_ref, h_ref, g, scale, shift, rows, chunk=128):
    def body(r, _):
        r0 = pl.multiple_of(r * chunk, chunk)
        x = x_ref[pl.ds(r0, chunk), :]
        ms = jnp.mean(x * x, axis=-1, keepdims=True)
        y = x * lax.rsqrt(ms + EPS) * g
        h_ref[pl.ds(r0, chunk), :] = (y * (1.0 + scale) + shift).astype(h_ref.dtype)
        return 0
    lax.fori_loop(0, rows // chunk, body, 0)


def _inproj_kernel(x_ref, mod_ref, g_ref, w_ref, o_ref, h_s):
    @pl.when(pl.program_id(1) == 0)
    def _():
        _norm_rows(x_ref, h_s, g_ref[...], mod_ref[0, 1:2, :], mod_ref[0, 0:1, :],
                   x_ref.shape[0])
    o_ref[...] = jnp.dot(h_s[...], w_ref[...], preferred_element_type=F32).astype(o_ref.dtype)


def _inproj(x2, mod3, g1, w, seq, tm=1024, tn=1536):
    t, d = x2.shape
    n = w.shape[1]
    tpb = seq // tm
    return pl.pallas_call(
        _inproj_kernel,
        grid=(t // tm, n // tn),
        in_specs=[pl.BlockSpec((tm, d), lambda i, j: (i, 0)),
                  pl.BlockSpec((1,) + mod3.shape[1:], lambda i, j: (i // tpb, 0, 0)),
                  pl.BlockSpec((1, d), lambda i, j: (0, 0)),
                  pl.BlockSpec((d, tn), lambda i, j: (0, j))],
        out_specs=pl.BlockSpec((tm, tn), lambda i, j: (i, j)),
        out_shape=jax.ShapeDtypeStruct((t, n), BF16),
        scratch_shapes=[pltpu.VMEM((tm, d), BF16)],
        compiler_params=_params(("arbitrary", "arbitrary")),
        name="inproj",
    )(x2, mod3, g1, w)


def _scan_up(a_ref, b_ref, n, p_ref, l_ref, a2_ref, b2_ref, grp):
    m = n // SCAN_FAN
    for g0 in range(0, m, grp):
        p = l = None
        for k in range(SCAN_FAN):
            ak = a_ref[pl.ds(g0 * SCAN_FAN + k, grp, stride=SCAN_FAN), :]
            bk = b_ref[pl.ds(g0 * SCAN_FAN + k, grp, stride=SCAN_FAN), :]
            if k == 0:
                p, l = ak, bk
            else:
                l = ak * l + bk
                p = ak * p
            p_ref[k, g0:g0 + grp, :] = p
            l_ref[k, g0:g0 + grp, :] = l
        a2_ref[g0:g0 + grp, :] = p
        b2_ref[g0:g0 + grp, :] = l


def _scan_down(p_ref, l_ref, hin_ref, m, out_ref, grp, inclusive):
    for g0 in range(0, m, grp):
        e = hin_ref[g0:g0 + grp, :]
        for k in range(SCAN_FAN):
            if inclusive:
                v = l_ref[k, g0:g0 + grp, :] + p_ref[k, g0:g0 + grp, :] * e
            elif k == 0:
                v = e
            else:
                v = l_ref[k - 1, g0:g0 + grp, :] + p_ref[k - 1, g0:g0 + grp, :] * e
            out_ref[pl.ds(g0 * SCAN_FAN + k, grp, stride=SCAN_FAN), :] = v


def _rglru_kernel(x_ref, gt_ref, cw_ref, cb_ref, wg_ref, ba_ref, bx_ref, lam_ref, o_ref,
                  xbuf, a_s, b_s, p1, l1, a2, b2, hin1, p2, l2, a3, b3, e3, *, chunk):
    seq = x_ref.shape[0]
    n2 = seq // SCAN_FAN
    n3 = n2 // SCAN_FAN

    xbuf[0:SUBLANES, :] = jnp.zeros((SUBLANES, LANES), F32)
    for r0 in range(0, seq, chunk):
        xbuf[SUBLANES + r0:SUBLANES + r0 + chunk, :] = x_ref[r0:r0 + chunk, :].astype(F32)

    lam = lam_ref[...]
    neg = -lam
    softplus_neg_lam = jnp.maximum(neg, 0.0) + jnp.log1p(jnp.exp(-jnp.abs(neg)))
    half_coef = 0.5 * C_RG * softplus_neg_lam
    half_ba = 0.5 * ba_ref[...]
    half_bx = 0.5 * bx_ref[...]

    for r0 in range(0, seq, chunk):
        xr = cb_ref[...]
        for j in range(RNN_CONV_W):
            off = SUBLANES + r0 - (RNN_CONV_W - 1 - j)
            xr = xr + cw_ref[j:j + 1, :] * xbuf[off:off + chunk, :]
        g = jnp.dot(xr.astype(BF16), wg_ref[0], preferred_element_type=F32)
        tr = jnp.tanh(g[:, :LANES] + half_ba)
        i = 0.5 + 0.5 * jnp.tanh(g[:, LANES:] + half_bx)
        y = half_coef + half_coef * tr
        a_s[r0:r0 + chunk, :] = jnp.exp2(y * -LOG2_E)
        t = jnp.tanh(y)
        b_s[r0:r0 + chunk, :] = lax.rsqrt(0.5 + 0.5 / t) * (i * xr)

    _scan_up(a_s, b_s, seq, p1, l1, a2, b2, grp=64)
    _scan_up(a2, b2, n2, p2, l2, a3, b3, grp=n3)
    h = jnp.zeros((1, LANES), F32)
    for r in range(n3):
        e3[r:r + 1, :] = h
        h = a3[r:r + 1, :] * h + b3[r:r + 1, :]
    _scan_down(p2, l2, e3, n3, hin1, grp=n3, inclusive=False)
    _scan_down(p1, l1, hin1, n2, b_s, grp=64, inclusive=True)

    for r0 in range(0, seq, chunk):
        gt = gt_ref[r0:r0 + chunk, :].astype(F32)
        o_ref[r0:r0 + chunk, :] = (jax.nn.gelu(gt) * b_s[r0:r0 + chunk, :]).astype(o_ref.dtype)


def _rglru(p, cw, cb, wg, ba, bx, lam, bsz, seq, x_col, gate_col, chunk=256):
    width = cw.shape[1]
    nblk = width // LANES
    n2 = seq // SCAN_FAN
    n3 = n2 // SCAN_FAN
    vec = lambda: pl.BlockSpec((1, LANES), lambda b, c: (0, c))
    return pl.pallas_call(
        functools.partial(_rglru_kernel, chunk=chunk),
        grid=(bsz, nblk),
        in_specs=[pl.BlockSpec((seq, LANES), lambda b, c: (b, x_col + c)),
                  pl.BlockSpec((seq, LANES), lambda b, c: (b, gate_col + c)),
                  pl.BlockSpec((RNN_CONV_W, LANES), lambda b, c: (0, c)),
                  vec(),
                  pl.BlockSpec((1, LANES, 2 * LANES), lambda b, c: (c, 0, 0)),
                  vec(), vec(), vec()],
        out_specs=pl.BlockSpec((seq, LANES), lambda b, c: (b, c)),
        out_shape=jax.ShapeDtypeStruct((bsz * seq, width), BF16),
        scratch_shapes=[pltpu.VMEM((seq + SUBLANES, LANES), F32),
                        pltpu.VMEM((seq, LANES), F32), pltpu.VMEM((seq, LANES), F32),
                        pltpu.VMEM((SCAN_FAN, n2, LANES), F32), pltpu.VMEM((SCAN_FAN, n2, LANES), F32),
                        pltpu.VMEM((n2, LANES), F32), pltpu.VMEM((n2, LANES), F32),
                        pltpu.VMEM((n2, LANES), F32),
                        pltpu.VMEM((SCAN_FAN, n3, LANES), F32), pltpu.VMEM((SCAN_FAN, n3, LANES), F32),
                        pltpu.VMEM((n3, LANES), F32), pltpu.VMEM((n3, LANES), F32),
                        pltpu.VMEM((n3, LANES), F32)],
        compiler_params=_params(("arbitrary", "arbitrary")),
        name="rglru",
    )(p, p, cw, cb, wg, ba, bx, lam)


def _attn_kernel(q_ref, k_ref, v_ref, gq_ref, gk_ref, uu_ref, o_ref, qn_s, kn_s, acc_s, run_s):
    seq = q_ref.shape[0]
    dh = gq_ref.shape[1]
    scale = dh ** -0.5 * LOG2_E

    def norm_body(r, _):
        r0 = pl.multiple_of(r * ATT_TQ, ATT_TQ)
        for h in range(ATT_HEADS):
            hs = slice(h * dh, (h + 1) * dh)
            q = q_ref[pl.ds(r0, ATT_TQ), hs].astype(F32)
            qn = q * lax.rsqrt(jnp.mean(q * q, axis=-1, keepdims=True) + EPS) * gq_ref[...]
            qn_s[h, pl.ds(r0, ATT_TQ), :] = (qn * scale).astype(BF16)
            k = k_ref[pl.ds(r0, ATT_TQ), hs].astype(F32)
            kn = k * lax.rsqrt(jnp.mean(k * k, axis=-1, keepdims=True) + EPS) * gk_ref[...]
            kn_s[h, pl.ds(r0, ATT_TQ), :] = kn.astype(BF16)
        return 0
    lax.fori_loop(0, seq // ATT_TQ, norm_body, 0)

    nsub = ATT_TQ // ATT_TK

    heads = range(ATT_HEADS)

    def block(q0, k0, masked):
        zs = [lax.dot_general(qn_s[h, pl.ds(q0, ATT_TQ), :], kn_s[h, pl.ds(k0, ATT_TQ), :],
                              (((1,), (1,)), ((), ())), preferred_element_type=F32)
              for h in heads]
        if masked:
            tri = (lax.broadcasted_iota(jnp.int32, (ATT_TK, ATT_TK), 1)
                   < lax.broadcasted_iota(jnp.int32, (ATT_TK, ATT_TK), 0))

        def on_diag_rows(x, fill):
            top = jnp.where(tri, x[:ATT_TK], fill)
            return top if x.shape[0] == ATT_TK else jnp.concatenate([top, x[ATT_TK:]], axis=0)

        def neg_log_keep(z):
            neg_abs = pltpu.bitcast(pltpu.bitcast(z, jnp.uint32) | jnp.uint32(0x80000000), F32)
            return jnp.maximum(z, 0.0) + jnp.log2(1.0 + jnp.exp2(neg_abs))

        runs = [run_s[h] for h in heads]
        ws = [[None] * nsub for _ in heads]
        if not masked:
            ss = [neg_log_keep(zs[h]) for h in heads]
        for j in reversed(range(nsub)):
            sl = slice(j * ATT_TK, (j + 1) * ATT_TK)
            r0 = j * ATT_TK if masked else 0
            for h in heads:
                z = zs[h][r0:, sl]
                s = on_diag_rows(neg_log_keep(z), 0.0) if masked else ss[h][:, sl]
                cs = jnp.dot(s.astype(BF16), uu_ref[...], preferred_element_type=F32)
                run = runs[h][r0:]
                w = jnp.exp2(z + cs + jnp.concatenate([run] * (ATT_TK // LANES), axis=1))
                if masked:
                    w = on_diag_rows(w, 0.0)
                w = w.astype(BF16)
                run = run - jnp.sum(s, axis=-1, keepdims=True)
                if r0:
                    w = jnp.concatenate([jnp.zeros((r0, ATT_TK), BF16), w], axis=0)
                    run = jnp.concatenate([runs[h][:r0], run], axis=0)
                ws[h][j] = w
                runs[h] = run
        for h in heads:
            run_s[h] = runs[h]
            vblk = v_ref[pl.ds(k0, ATT_TQ), h * dh:(h + 1) * dh]
            acc_s[h] += jnp.dot(jnp.concatenate(ws[h], axis=1), vblk, preferred_element_type=F32)

    def q_body(qi, _):
        q0 = pl.multiple_of(qi * ATT_TQ, ATT_TQ)
        acc_s[...] = jnp.zeros_like(acc_s)
        run_s[...] = jnp.zeros_like(run_s)
        block(q0, q0, True)

        def k_body(j, _):
            block(q0, pl.multiple_of(q0 - (j + 1) * ATT_TQ, ATT_TQ), False)
            return 0
        lax.fori_loop(0, qi, k_body, 0)
        for h in range(ATT_HEADS):
            o_ref[pl.ds(q0, ATT_TQ), h * dh:(h + 1) * dh] = acc_s[h].astype(o_ref.dtype)
        return 0
    lax.fori_loop(0, seq // ATT_TQ, q_body, 0)


def _attn(p, gq, gk, uu, bsz, seq, n_heads, dh, q_col, k_col, v_col):
    wblk = ATT_HEADS * dh
    return pl.pallas_call(
        _attn_kernel,
        grid=(bsz, n_heads // ATT_HEADS),
        in_specs=[pl.BlockSpec((seq, wblk), lambda b, h: (b, q_col + h)),
                  pl.BlockSpec((seq, wblk), lambda b, h: (b, k_col + h)),
                  pl.BlockSpec((seq, wblk), lambda b, h: (b, v_col + h)),
                  pl.BlockSpec((1, dh), lambda b, h: (0, 0)),
                  pl.BlockSpec((1, dh), lambda b, h: (0, 0)),
                  pl.BlockSpec(uu.shape, lambda b, h: (0, 0))],
        out_specs=pl.BlockSpec((seq, wblk), lambda b, h: (b, h)),
        out_shape=jax.ShapeDtypeStruct((bsz * seq, n_heads * dh), BF16),
        scratch_shapes=[pltpu.VMEM((ATT_HEADS, seq, dh), BF16),
                        pltpu.VMEM((ATT_HEADS, seq, dh), BF16),
                        pltpu.VMEM((ATT_HEADS, ATT_TQ, dh), F32),
                        pltpu.VMEM((ATT_HEADS, ATT_TQ, LANES), F32)],
        compiler_params=_params(("arbitrary", "arbitrary")),
        name="attn",
    )(p, p, p, gq, gk, uu)


def _merge_kernel(ya_ref, ob_ref, gr0_ref, gr1_ref, gs0_ref, gs1_ref, x_ref, mod_ref, g2_ref,
                  wr_ref, wsb_ref, wo_ref, x1_ref, h2_ref, mg_s, *, nc):
    d = x_ref.shape[1]
    half = d // 2
    for n0 in range(0, d, nc):
        yr = jnp.dot(ya_ref[...], wr_ref[:, n0:n0 + nc], preferred_element_type=F32)
        ys = jnp.dot(ob_ref[...], wsb_ref[:, n0:n0 + nc], preferred_element_type=F32)
        gr_ref, gs_ref = (gr0_ref, gs0_ref) if n0 < half else (gr1_ref, gs1_ref)
        m0 = n0 % half
        gr = jax.nn.sigmoid(gr_ref[:, m0:m0 + nc].astype(F32))
        gs = jax.nn.sigmoid(gs_ref[:, m0:m0 + nc].astype(F32))
        mg_s[:, n0:n0 + nc] = (gr * yr + gs * ys).astype(BF16)
    for n0 in range(0, d, nc):
        y = jnp.dot(mg_s[...], wo_ref[:, n0:n0 + nc], preferred_element_type=F32)
        x1_ref[:, n0:n0 + nc] = x_ref[:, n0:n0 + nc] + mod_ref[0, 2:3, n0:n0 + nc] * y
    _norm_rows(x1_ref, h2_ref, g2_ref[...], mod_ref[0, 4:5, :], mod_ref[0, 3:4, :],
               x1_ref.shape[0])


def _merge(ya, ob, p, x2, mod3, g2, wr, wsb, wo, seq, gm_col, tm=512, nc=512):
    t, d = x2.shape
    w = ya.shape[1]
    half = d // 2
    tpb = seq // tm
    const = lambda shape: pl.BlockSpec(shape, lambda i: (0, 0), pipeline_mode=pl.Buffered(1))
    gate = lambda k: pl.BlockSpec((tm, half), lambda i: (i, gm_col + k))
    return pl.pallas_call(
        functools.partial(_merge_kernel, nc=nc),
        grid=(t // tm,),
        in_specs=[pl.BlockSpec((tm, w), lambda i: (i, 0)),
                  pl.BlockSpec((tm, w), lambda i: (i, 0)),
                  gate(0), gate(1), gate(2), gate(3),
                  pl.BlockSpec((tm, d), lambda i: (i, 0)),
                  pl.BlockSpec((1,) + mod3.shape[1:], lambda i: (i // tpb, 0, 0)),
                  pl.BlockSpec((1, d), lambda i: (0, 0)),
                  const(wr.shape), const(wsb.shape), const(wo.shape)],
        out_specs=[pl.BlockSpec((tm, d), lambda i: (i, 0)),
                   pl.BlockSpec((tm, d), lambda i: (i, 0))],
        out_shape=[jax.ShapeDtypeStruct((t, d), F32), jax.ShapeDtypeStruct((t, d), BF16)],
        scratch_shapes=[pltpu.VMEM((tm, d), BF16)],
        compiler_params=_params(("arbitrary",)),
        name="merge",
    )(ya, ob, p, p, p, p, x2, mod3, g2, wr, wsb, wo)


def _ffn_kernel(h_ref, x1_ref, mod_ref, wv_ref, wg_ref, cw_ref, cb_ref, wd_ref, o_ref,
                ubuf, halo, act_s, *, tiles_per_seq, sub, nsub):
    i = pl.program_id(0)
    f = pl.program_id(1)
    tm = h_ref.shape[0]
    tf = wv_ref.shape[1]
    d = o_ref.shape[1]

    @pl.when(f == 0)
    def _():
        o_ref[...] = jnp.zeros_like(o_ref)

    @pl.when(i % tiles_per_seq == 0)
    def _():
        ubuf[0:SUBLANES, :] = jnp.zeros((SUBLANES, tf), F32)

    @pl.when(i % tiles_per_seq != 0)
    def _():
        ubuf[0:SUBLANES, :] = halo[f]

    h = h_ref[...]
    for c0 in range(0, tf, sub):
        cs = slice(c0, c0 + sub)
        val = jnp.dot(h, wv_ref[:, cs], preferred_element_type=F32)
        u = jnp.dot(h, wg_ref[:, cs], preferred_element_type=F32)
        ubuf[SUBLANES:, cs] = u
        g = cb_ref[:, cs] + cw_ref[FFN_CONV_W - 1:FFN_CONV_W, cs] * u
        for j in range(FFN_CONV_W - 1):
            off = SUBLANES - (FFN_CONV_W - 1 - j)
            g = g + cw_ref[j:j + 1, cs] * ubuf[off:off + tm, cs]
        act_s[:, cs] = (jax.nn.gelu(g) * val).astype(BF16)
    halo[f] = ubuf[tm:tm + SUBLANES, :]

    for n0 in range(0, d, nsub):
        o_ref[:, n0:n0 + nsub] += jnp.dot(act_s[...], wd_ref[:, n0:n0 + nsub],
                                          preferred_element_type=F32)

    @pl.when(f == pl.num_programs(1) - 1)
    def _():
        o_ref[...] = x1_ref[...] + mod_ref[0, 5:6, :] * o_ref[...]


def _ffn(h2, x1, mod3, w_up, cw, cb, wd, seq, tm=512, tf=1024, sub=256, nsub=512):
    t, d = x1.shape
    dff = w_up.shape[1] // 2
    tpb = seq // tm
    nf = dff // tf
    return pl.pallas_call(
        functools.partial(_ffn_kernel, tiles_per_seq=tpb, sub=sub, nsub=nsub),
        grid=(t // tm, nf),
        in_specs=[pl.BlockSpec((tm, d), lambda i, f: (i, 0)),
                  pl.BlockSpec((tm, d), lambda i, f: (i, 0)),
                  pl.BlockSpec((1,) + mod3.shape[1:], lambda i, f: (i // tpb, 0, 0)),
                  pl.BlockSpec((d, tf), lambda i, f: (0, f)),
                  pl.BlockSpec((d, tf), lambda i, f: (0, nf + f)),
                  pl.BlockSpec((FFN_CONV_W, tf), lambda i, f: (0, f)),
                  pl.BlockSpec((1, tf), lambda i, f: (0, f)),
                  pl.BlockSpec((tf, d), lambda i, f: (f, 0))],
        out_specs=pl.BlockSpec((tm, d), lambda i, f: (i, 0)),
        out_shape=jax.ShapeDtypeStruct((t, d), F32),
        scratch_shapes=[pltpu.VMEM((tm + SUBLANES, tf), F32),
                        pltpu.VMEM((nf, SUBLANES, tf), F32),
                        pltpu.VMEM((tm, tf), BF16)],
        compiler_params=_params(("arbitrary", "arbitrary")),
        name="ffn",
    )(h2, x1, mod3, w_up, w_up, cw, cb, wd)


def _block_diag_pairs(w):
    nh, hd, _ = w.shape
    w = w.reshape(nh // 2, 2, hd, hd)
    z = jnp.zeros_like(w[:, 0])
    top = jnp.concatenate([w[:, 0], z], axis=2)
    bot = jnp.concatenate([z, w[:, 1]], axis=2)
    return jnp.concatenate([top, bot], axis=1)


def _cumsum_matrix():
    j = jnp.arange(ATT_TK)[:, None]
    s = jnp.arange(ATT_TK)[None, :]
    return -(j >= s).astype(BF16)


def kernel(x, c, w_ada, b_ada, g_norm1, w_in, conv_rnn_w, conv_rnn_b, w_rg_a, b_rg_a, w_rg_x,
           b_rg_x, lru_lambda, g_q, g_k, w_proj_rnn, w_proj_sb, w_out, g_norm2, w_up,
           conv_ffn_w, conv_ffn_b, w_down):
    bsz, seq, d = x.shape
    depth = w_ada.shape[0]
    rnn_w = conv_rnn_w.shape[2]
    dh = g_q.shape[1]
    sb_w = w_proj_sb.shape[1]
    dff = conv_ffn_w.shape[2]
    gate_col = rnn_w // LANES
    q_off = 2 * rnn_w
    att_blk = ATT_HEADS * dh
    q_col = q_off // att_blk
    k_col = (q_off + sb_w) // att_blk
    v_col = (q_off + 2 * sb_w) // att_blk
    gm_col = (q_off + 3 * sb_w) // (d // 2)
    assert dh == LANES and q_off % att_blk == 0 and sb_w % att_blk == 0
    assert (q_off + 3 * sb_w) % (d // 2) == 0
    uu = _cumsum_matrix()

    x2 = x.reshape(bsz * seq, d)
    for l in range(depth):
        mod3 = _ada(c, w_ada[l], b_ada[l]).reshape(bsz, 6, d)
        p = _inproj(x2, mod3, g_norm1[l].reshape(1, d), w_in[l].astype(BF16), seq)

        wg = (0.5 * jnp.concatenate([_block_diag_pairs(w_rg_a[l]), _block_diag_pairs(w_rg_x[l])],
                                    axis=2)).astype(BF16)
        ya = _rglru(p, conv_rnn_w[l], conv_rnn_b[l].reshape(1, rnn_w), wg,
                    b_rg_a[l].reshape(1, rnn_w), b_rg_x[l].reshape(1, rnn_w),
                    lru_lambda[l].reshape(1, rnn_w), bsz, seq, 0, gate_col)
        ob = _attn(p, g_q[l].reshape(1, dh), g_k[l].reshape(1, dh), uu, bsz, seq,
                   sb_w // dh, dh, q_col, k_col, v_col)
        x1, h2 = _merge(ya, ob, p, x2, mod3, g_norm2[l].reshape(1, d),
                        w_proj_rnn[l].astype(BF16), w_proj_sb[l].astype(BF16),
                        w_out[l].astype(BF16), seq, gm_col)
        x2 = _ffn(h2, x1, mod3, w_up[l].astype(BF16), conv_ffn_w[l],
                  conv_ffn_b[l].reshape(1, dff), w_down[l].astype(BF16), seq)
    return x2.reshape(bsz, seq, d)
```

```python
import functools

import jax
import jax.numpy as jnp
from jax import lax
from jax.experimental import pallas as pl
from jax.experimental.pallas import tpu as pltpu

F32 = jnp.float32
BF16 = jnp.bfloat16

EPS = 1e-6
C_RG = 8.0
N_RNN_HEADS = 16
N_SB_HEADS = 8
RNN_CONV_W = 4
FFN_CONV_W = 3

LANES = 128
SUBLANES = 8
VMEM_LIMIT = 56 * 1024 * 1024

SCAN_FAN = 8
ATT_TQ = 512
ATT_TK = 256
ATT_HEADS = 2
LOG2_E = 1.4426950408889634


def _params(sem, vmem=VMEM_LIMIT, flags=None):
    return pltpu.CompilerParams(dimension_semantics=sem, vmem_limit_bytes=vmem, flags=flags)


def _ada_kernel(c_ref, w_ref, b_ref, o_ref):
    bsz = c_ref.shape[0]
    a = jax.nn.silu(c_ref[...])
    a_hi = a.astype(BF16)
    a_lo = (a - a_hi.astype(F32)).astype(BF16)
    w = w_ref[...]
    w_hi = w.astype(BF16)
    w_lo = (w - w_hi.astype(F32)).astype(BF16)
    r = jnp.dot(jnp.concatenate([a_hi, a_lo], axis=0), w_hi, preferred_element_type=F32)
    o_ref[...] = (r[:bsz] + r[bsz:] + jnp.dot(a_hi, w_lo, preferred_element_type=F32)
                  + b_ref[...])


def _ada(c, w, b, tn=1024):
    bsz, d = c.shape
    n = w.shape[1]
    return pl.pallas_call(
        _ada_kernel,
        grid=(n // tn,),
        in_specs=[pl.BlockSpec((bsz, d), lambda j: (0, 0)),
                  pl.BlockSpec((d, tn), lambda j: (0, j)),
                  pl.BlockSpec((1, tn), lambda j: (0, j))],
        out_specs=pl.BlockSpec((bsz, tn), lambda j: (0, j)),
        out_shape=jax.ShapeDtypeStruct((bsz, n), F32),
        compiler_params=_params(("arbitrary",)),
        name="ada",
    )(c, w, b.reshape(1, n))


def _norm_rows(x_ref, h_ref, g, scale, shift, rows, chunk=128):
    def body(r, _):
        r0 = pl.multiple_of(r * chunk, chunk)
        x = x_ref[pl.ds(r0, chunk), :]
        ms = jnp.mean(x * x, axis=-1, keepdims=True)
        y = x * lax.rsqrt(ms + EPS) * g
        h_ref[pl.ds(r0, chunk), :] = (y * (1.0 + scale) + shift).astype(h_ref.dtype)
        return 0
    lax.fori_loop(0, rows // chunk, body, 0)


def _inproj_kernel(x_ref, mod_ref, g_ref, w_ref, o_ref, h_s):
    @pl.when(pl.program_id(1) == 0)
    def _():
        _norm_rows(x_ref, h_s, g_ref[...], mod_ref[0, 1:2, :], mod_ref[0, 0:1, :],
                   x_ref.shape[0])
    o_ref[...] = jnp.dot(h_s[...], w_ref[...], preferred_element_type=F32).astype(o_ref.dtype)


def _inproj(x2, mod3, g1, w, seq, tm=1024, tn=1536):
    t, d = x2.shape
    n = w.shape[1]
    tpb = seq // tm
    return pl.pallas_call(
        _inproj_kernel,
        grid=(t // tm, n // tn),
        in_specs=[pl.BlockSpec((tm, d), lambda i, j: (i, 0)),
                  pl.BlockSpec((1,) + mod3.shape[1:], lambda i, j: (i // tpb, 0, 0)),
                  pl.BlockSpec((1, d), lambda i, j: (0, 0)),
                  pl.BlockSpec((d, tn), lambda i, j: (0, j))],
        out_specs=pl.BlockSpec((tm, tn), lambda i, j: (i, j)),
        out_shape=jax.ShapeDtypeStruct((t, n), BF16),
        scratch_shapes=[pltpu.VMEM((tm, d), BF16)],
        compiler_params=_params(("arbitrary", "arbitrary")),
        name="inproj",
    )(x2, mod3, g1, w)


def _scan_up(a_ref, b_ref, n, p_ref, l_ref, a2_ref, b2_ref, grp):
    m = n // SCAN_FAN
    for g0 in range(0, m, grp):
        p = l = None
        for k in range(SCAN_FAN):
            ak = a_ref[pl.ds(g0 * SCAN_FAN + k, grp, stride=SCAN_FAN), :]
            bk = b_ref[pl.ds(g0 * SCAN_FAN + k, grp, stride=SCAN_FAN), :]
            if k == 0:
                p, l = ak, bk
            else:
                l = ak * l + bk
                p = ak * p
            p_ref[k, g0:g0 + grp, :] = p
            l_ref[k, g0:g0 + grp, :] = l
        a2_ref[g0:g0 + grp, :] = p
        b2_ref[g0:g0 + grp, :] = l


def _scan_down(p_ref, l_ref, hin_ref, m, out_ref, grp, inclusive):
    for g0 in range(0, m, grp):
        e = hin_ref[g0:g0 + grp, :]
        for k in range(SCAN_FAN):
            if inclusive:
                v = l_ref[k, g0:g0 + grp, :] + p_ref[k, g0:g0 + grp, :] * e
            elif k == 0:
                v = e
            else:
                v = l_ref[k - 1, g0:g0 + grp, :] + p_ref[k - 1, g0:g0 + grp, :] * e
            out_ref[pl.ds(g0 * SCAN_FAN + k, grp, stride=SCAN_FAN), :] = v


def _rglru_kernel(x_ref, gt_ref, cw_ref, cb_ref, wg_ref, ba_ref, bx_ref, lam_ref, o_ref,
                  xbuf, a_s, b_s, p1, l1, a2, b2, hin1, p2, l2, a3, b3, e3, *, chunk):
    seq = x_ref.shape[0]
    n2 = seq // SCAN_FAN
    n3 = n2 // SCAN_FAN

    xbuf[0:SUBLANES, :] = jnp.zeros((SUBLANES, LANES), F32)
    for r0 in range(0, seq, chunk):
        xbuf[SUBLANES + r0:SUBLANES + r0 + chunk, :] = x_ref[r0:r0 + chunk, :].astype(F32)

    lam = lam_ref[...]
    neg = -lam
    softplus_neg_lam = jnp.maximum(neg, 0.0) + jnp.log1p(jnp.exp(-jnp.abs(neg)))
    half_coef = 0.5 * C_RG * softplus_neg_lam
    half_ba = 0.5 * ba_ref[...]
    half_bx = 0.5 * bx_ref[...]

    for r0 in range(0, seq, chunk):
        xr = cb_ref[...]
        for j in range(RNN_CONV_W):
            off = SUBLANES + r0 - (RNN_CONV_W - 1 - j)
            xr = xr + cw_ref[j:j + 1, :] * xbuf[off:off + chunk, :]
        g = jnp.dot(xr.astype(BF16), wg_ref[0], preferred_element_type=F32)
        tr = jnp.tanh(g[:, :LANES] + half_ba)
        i = 0.5 + 0.5 * jnp.tanh(g[:, LANES:] + half_bx)
        y = half_coef + half_coef * tr
        a_s[r0:r0 + chunk, :] = jnp.exp2(y * -LOG2_E)
        t = jnp.tanh(y)
        b_s[r0:r0 + chunk, :] = lax.rsqrt(0.5 + 0.5 / t) * (i * xr)

    _scan_up(a_s, b_s, seq, p1, l1, a2, b2, grp=64)
    _scan_up(a2, b2, n2, p2, l2, a3, b3, grp=n3)
    h = jnp.zeros((1, LANES), F32)
    for r in range(n3):
        e3[r:r + 1, :] = h
        h = a3[r:r + 1, :] * h + b3[r:r + 1, :]
    _scan_down(p2, l2, e3, n3, hin1, grp=n3, inclusive=False)
    _scan_down(p1, l1, hin1, n2, b_s, grp=64, inclusive=True)

    for r0 in range(0, seq, chunk):
        gt = gt_ref[r0:r0 + chunk, :].astype(F32)
        o_ref[r0:r0 + chunk, :] = (jax.nn.gelu(gt) * b_s[r0:r0 + chunk, :]).astype(o_ref.dtype)


def _rglru(p, cw, cb, wg, ba, bx, lam, bsz, seq, x_col, gate_col, chunk=256):
    width = cw.shape[1]
    nblk = width // LANES
    n2 = seq // SCAN_FAN
    n3 = n2 // SCAN_FAN
    vec = lambda: pl.BlockSpec((1, LANES), lambda b, c: (0, c))
    return pl.pallas_call(
        functools.partial(_rglru_kernel, chunk=chunk),
        grid=(bsz, nblk),
        in_specs=[pl.BlockSpec((seq, LANES), lambda b, c: (b, x_col + c)),
                  pl.BlockSpec((seq, LANES), lambda b, c: (b, gate_col + c)),
                  pl.BlockSpec((RNN_CONV_W, LANES), lambda b, c: (0, c)),
                  vec(),
                  pl.BlockSpec((1, LANES, 2 * LANES), lambda b, c: (c, 0, 0)),
                  vec(), vec(), vec()],
        out_specs=pl.BlockSpec((seq, LANES), lambda b, c: (b, c)),
        out_shape=jax.ShapeDtypeStruct((bsz * seq, width), BF16),
        scratch_shapes=[pltpu.VMEM((seq + SUBLANES, LANES), F32),
                        pltpu.VMEM((seq, LANES), F32), pltpu.VMEM((seq, LANES), F32),
                        pltpu.VMEM((SCAN_FAN, n2, LANES), F32), pltpu.VMEM((SCAN_FAN, n2, LANES), F32),
                        pltpu.VMEM((n2, LANES), F32), pltpu.VMEM((n2, LANES), F32),
                        pltpu.VMEM((n2, LANES), F32),
                        pltpu.VMEM((SCAN_FAN, n3, LANES), F32), pltpu.VMEM((SCAN_FAN, n3, LANES), F32),
                        pltpu.VMEM((n3, LANES), F32), pltpu.VMEM((n3, LANES), F32),
                        pltpu.VMEM((n3, LANES), F32)],
        compiler_params=_params(("arbitrary", "arbitrary")),
        name="rglru",
    )(p, p, cw, cb, wg, ba, bx, lam)


def _attn_kernel(q_ref, k_ref, v_ref, gq_ref, gk_ref, uu_ref, o_ref, qn_s, kn_s, acc_s, run_s):
    seq = q_ref.shape[0]
    dh = gq_ref.shape[1]
    scale = dh ** -0.5 * LOG2_E

    def norm_body(r, _):
        r0 = pl.multiple_of(r * ATT_TQ, ATT_TQ)
        for h in range(ATT_HEADS):
            hs = slice(h * dh, (h + 1) * dh)
            q = q_ref[pl.ds(r0, ATT_TQ), hs].astype(F32)
            qn = q * lax.rsqrt(jnp.mean(q * q, axis=-1, keepdims=True) + EPS) * gq_ref[...]
            qn_s[h, pl.ds(r0, ATT_TQ), :] = (qn * scale).astype(BF16)
            k = k_ref[pl.ds(r0, ATT_TQ), hs].astype(F32)
            kn = k * lax.rsqrt(jnp.mean(k * k, axis=-1, keepdims=True) + EPS) * gk_ref[...]
            kn_s[h, pl.ds(r0, ATT_TQ), :] = kn.astype(BF16)
        return 0
    lax.fori_loop(0, seq // ATT_TQ, norm_body, 0)

    nsub = ATT_TQ // ATT_TK

    heads = range(ATT_HEADS)

    def block(q0, k0, masked):
        zs = [lax.dot_general(qn_s[h, pl.ds(q0, ATT_TQ), :], kn_s[h, pl.ds(k0, ATT_TQ), :],
                              (((1,), (1,)), ((), ())), preferred_element_type=F32)
              for h in heads]
        if masked:
            tri = (lax.broadcasted_iota(jnp.int32, (ATT_TK, ATT_TK), 1)
                   < lax.broadcasted_iota(jnp.int32, (ATT_TK, ATT_TK), 0))

        def on_diag_rows(x, fill):
            top = jnp.where(tri, x[:ATT_TK], fill)
            return top if x.shape[0] == ATT_TK else jnp.concatenate([top, x[ATT_TK:]], axis=0)

        def neg_log_keep(z):
            neg_abs = pltpu.bitcast(pltpu.bitcast(z, jnp.uint32) | jnp.uint32(0x80000000), F32)
            return jnp.maximum(z, 0.0) + jnp.log2(1.0 + jnp.exp2(neg_abs))

        runs = [run_s[h] for h in heads]
        ws = [[None] * nsub for _ in heads]
        if not masked:
            ss = [neg_log_keep(zs[h]) for h in heads]
        for j in reversed(range(nsub)):
            sl = slice(j * ATT_TK, (j + 1) * ATT_TK)
            r0 = j * ATT_TK if masked else 0
            for h in heads:
                z = zs[h][r0:, sl]
                s = on_diag_rows(neg_log_keep(z), 0.0) if masked else ss[h][:, sl]
                cs = jnp.dot(s.astype(BF16), uu_ref[...], preferred_element_type=F32)
                run = runs[h][r0:]
                w = jnp.exp2(z + cs + jnp.concatenate([run] * (ATT_TK // LANES), axis=1))
                if masked:
                    w = on_diag_rows(w, 0.0)
                w = w.astype(BF16)
                run = run - jnp.sum(s, axis=-1, keepdims=True)
                if r0:
                    w = jnp.concatenate([jnp.zeros((r0, ATT_TK), BF16), w], axis=0)
                    run = jnp.concatenate([runs[h][:r0], run], axis=0)
                ws[h][j] = w
                runs[h] = run
        for h in heads:
            run_s[h] = runs[h]
            vblk = v_ref[pl.ds(k0, ATT_TQ), h * dh:(h + 1) * dh]
            acc_s[h] += jnp.dot(jnp.concatenate(ws[h], axis=1), vblk, preferred_element_type=F32)

    def q_body(qi, _):
        q0 = pl.multiple_of(qi * ATT_TQ, ATT_TQ)
        acc_s[...] = jnp.zeros_like(acc_s)
        run_s[...] = jnp.zeros_like(run_s)
        block(q0, q0, True)

        def k_body(j, _):
            block(q0, pl.multiple_of(q0 - (j + 1) * ATT_TQ, ATT_TQ), False)
            return 0
        lax.fori_loop(0, qi, k_body, 0)
        for h in range(ATT_HEADS):
            o_ref[pl.ds(q0, ATT_TQ), h * dh:(h + 1) * dh] = acc_s[h].astype(o_ref.dtype)
        return 0
    lax.fori_loop(0, seq // ATT_TQ, q_body, 0)


def _attn(p, gq, gk, uu, bsz, seq, n_heads, dh, q_col, k_col, v_col):
    wblk = ATT_HEADS * dh
    return pl.pallas_call(
        _attn_kernel,
        grid=(bsz, n_heads // ATT_HEADS),
        in_specs=[pl.BlockSpec((seq, wblk), lambda b, h: (b, q_col + h)),
                  pl.BlockSpec((seq, wblk), lambda b, h: (b, k_col + h)),
                  pl.BlockSpec((seq, wblk), lambda b, h: (b, v_col + h)),
                  pl.BlockSpec((1, dh), lambda b, h: (0, 0)),
                  pl.BlockSpec((1, dh), lambda b, h: (0, 0)),
                  pl.BlockSpec(uu.shape, lambda b, h: (0, 0))],
        out_specs=pl.BlockSpec((seq, wblk), lambda b, h: (b, h)),
        out_shape=jax.ShapeDtypeStruct((bsz * seq, n_heads * dh), BF16),
        scratch_shapes=[pltpu.VMEM((ATT_HEADS, seq, dh), BF16),
                        pltpu.VMEM((ATT_HEADS, seq, dh), BF16),
                        pltpu.VMEM((ATT_HEADS, ATT_TQ, dh), F32),
                        pltpu.VMEM((ATT_HEADS, ATT_TQ, LANES), F32)],
        compiler_params=_params(("arbitrary", "arbitrary")),
        name="attn",
    )(p, p, p, gq, gk, uu)


def _merge_kernel(ya_ref, ob_ref, gr0_ref, gr1_ref, gs0_ref, gs1_ref, x_ref, mod_ref, g2_ref,
                  wr_ref, wsb_ref, wo_ref, x1_ref, h2_ref, mg_s, *, nc):
    d = x_ref.shape[1]
    half = d // 2
    for n0 in range(0, d, nc):
        yr = jnp.dot(ya_ref[...], wr_ref[:, n0:n0 + nc], preferred_element_type=F32)
        ys = jnp.dot(ob_ref[...], wsb_ref[:, n0:n0 + nc], preferred_element_type=F32)
        gr_ref, gs_ref = (gr0_ref, gs0_ref) if n0 < half else (gr1_ref, gs1_ref)
        m0 = n0 % half
        gr = jax.nn.sigmoid(gr_ref[:, m0:m0 + nc].astype(F32))
        gs = jax.nn.sigmoid(gs_ref[:, m0:m0 + nc].astype(F32))
        mg_s[:, n0:n0 + nc] = (gr * yr + gs * ys).astype(BF16)
    for n0 in range(0, d, nc):
        y = jnp.dot(mg_s[...], wo_ref[:, n0:n0 + nc], preferred_element_type=F32)
        x1_ref[:, n0:n0 + nc] = x_ref[:, n0:n0 + nc] + mod_ref[0, 2:3, n0:n0 + nc] * y
    _norm_rows(x1_ref, h2_ref, g2_ref[...], mod_ref[0, 4:5, :], mod_ref[0, 3:4, :],
               x1_ref.shape[0])


def _merge(ya, ob, p, x2, mod3, g2, wr, wsb, wo, seq, gm_col, tm=512, nc=512):
    t, d = x2.shape
    w = ya.shape[1]
    half = d // 2
    tpb = seq // tm
    const = lambda shape: pl.BlockSpec(shape, lambda i: (0, 0), pipeline_mode=pl.Buffered(1))
    gate = lambda k: pl.BlockSpec((tm, half), lambda i: (i, gm_col + k))
    return pl.pallas_call(
        functools.partial(_merge_kernel, nc=nc),
        grid=(t // tm,),
        in_specs=[pl.BlockSpec((tm, w), lambda i: (i, 0)),
                  pl.BlockSpec((tm, w), lambda i: (i, 0)),
                  gate(0), gate(1), gate(2), gate(3),
                  pl.BlockSpec((tm, d), lambda i: (i, 0)),
                  pl.BlockSpec((1,) + mod3.shape[1:], lambda i: (i // tpb, 0, 0)),
                  pl.BlockSpec((1, d), lambda i: (0, 0)),
                  const(wr.shape), const(wsb.shape), const(wo.shape)],
        out_specs=[pl.BlockSpec((tm, d), lambda i: (i, 0)),
                   pl.BlockSpec((tm, d), lambda i: (i, 0))],
        out_shape=[jax.ShapeDtypeStruct((t, d), F32), jax.ShapeDtypeStruct((t, d), BF16)],
        scratch_shapes=[pltpu.VMEM((tm, d), BF16)],
        compiler_params=_params(("arbitrary",)),
        name="merge",
    )(ya, ob, p, p, p, p, x2, mod3, g2, wr, wsb, wo)


def _ffn_kernel(h_ref, x1_ref, mod_ref, wv_ref, wg_ref, cw_ref, cb_ref, wd_ref, o_ref,
                ubuf, halo, act_s, val_s, *, tiles_per_seq, sub, nsub):
    i = pl.program_id(0)
    f = pl.program_id(1)
    tm = h_ref.shape[0]
    tf = wv_ref.shape[1]
    d = o_ref.shape[1]

    @pl.when(f == 0)
    def _():
        o_ref[...] = jnp.zeros_like(o_ref)

    @pl.when(i % tiles_per_seq == 0)
    def _():
        ubuf[0:SUBLANES, :] = jnp.zeros((SUBLANES, tf), F32)

    @pl.when(i % tiles_per_seq != 0)
    def _():
        ubuf[0:SUBLANES, :] = halo[f]

    h = h_ref[...]

    def up(c0):
        cs = slice(c0, c0 + sub)
        val_s[:, cs] = jnp.dot(h, wv_ref[:, cs], preferred_element_type=F32)
        ubuf[SUBLANES:, cs] = jnp.dot(h, wg_ref[:, cs], preferred_element_type=F32)

    def gate(c0):
        cs = slice(c0, c0 + sub)
        g = cb_ref[:, cs]
        for j in range(FFN_CONV_W):
            off = SUBLANES - (FFN_CONV_W - 1 - j)
            g = g + cw_ref[j:j + 1, cs] * ubuf[off:off + tm, cs]
        act_s[:, cs] = (jax.nn.gelu(g) * val_s[:, cs]).astype(BF16)

    def down(k0):
        ks = slice(k0, k0 + tf // 2)
        for n0 in range(0, d, nsub):
            o_ref[:, n0:n0 + nsub] += jnp.dot(act_s[:, ks], wd_ref[ks, n0:n0 + nsub],
                                              preferred_element_type=F32)

    up(0)
    for c0 in range(sub, tf, sub):
        up(c0)
        gate(c0 - sub)
    assert (tf // 2) % sub == 0 and tf // 2 <= tf - sub
    down(0)
    gate(tf - sub)
    halo[f] = ubuf[tm:tm + SUBLANES, :]
    down(tf // 2)

    @pl.when(f == pl.num_programs(1) - 1)
    def _():
        o_ref[...] = x1_ref[...] + mod_ref[0, 5:6, :] * o_ref[...]


def _ffn(h2, x1, mod3, w_up, cw, cb, wd, seq, tm=512, tf=1024, sub=256, nsub=512):
    t, d = x1.shape
    dff = w_up.shape[1] // 2
    tpb = seq // tm
    nf = dff // tf
    return pl.pallas_call(
        functools.partial(_ffn_kernel, tiles_per_seq=tpb, sub=sub, nsub=nsub),
        grid=(t // tm, nf),
        in_specs=[pl.BlockSpec((tm, d), lambda i, f: (i, 0)),
                  pl.BlockSpec((tm, d), lambda i, f: (i, 0)),
                  pl.BlockSpec((1,) + mod3.shape[1:], lambda i, f: (i // tpb, 0, 0)),
                  pl.BlockSpec((d, tf), lambda i, f: (0, f)),
                  pl.BlockSpec((d, tf), lambda i, f: (0, nf + f)),
                  pl.BlockSpec((FFN_CONV_W, tf), lambda i, f: (0, f)),
                  pl.BlockSpec((1, tf), lambda i, f: (0, f)),
                  pl.BlockSpec((tf, d), lambda i, f: (f, 0))],
        out_specs=pl.BlockSpec((tm, d), lambda i, f: (i, 0)),
        out_shape=jax.ShapeDtypeStruct((t, d), F32),
        scratch_shapes=[pltpu.VMEM((tm + SUBLANES, tf), F32),
                        pltpu.VMEM((nf, SUBLANES, tf), F32),
                        pltpu.VMEM((tm, tf), BF16),
                        pltpu.VMEM((tm, tf), F32)],
        compiler_params=_params(("arbitrary", "arbitrary")),
        name="ffn",
    )(h2, x1, mod3, w_up, w_up, cw, cb, wd)


def _block_diag_pairs(w):
    nh, hd, _ = w.shape
    w = w.reshape(nh // 2, 2, hd, hd)
    z = jnp.zeros_like(w[:, 0])
    top = jnp.concatenate([w[:, 0], z], axis=2)
    bot = jnp.concatenate([z, w[:, 1]], axis=2)
    return jnp.concatenate([top, bot], axis=1)


def _cumsum_matrix():
    j = jnp.arange(ATT_TK)[:, None]
    s = jnp.arange(ATT_TK)[None, :]
    return -(j >= s).astype(BF16)


def kernel(x, c, w_ada, b_ada, g_norm1, w_in, conv_rnn_w, conv_rnn_b, w_rg_a, b_rg_a, w_rg_x,
           b_rg_x, lru_lambda, g_q, g_k, w_proj_rnn, w_proj_sb, w_out, g_norm2, w_up,
           conv_ffn_w, conv_ffn_b, w_down):
    bsz, seq, d = x.shape
    depth = w_ada.shape[0]
    rnn_w = conv_rnn_w.shape[2]
    dh = g_q.shape[1]
    sb_w = w_proj_sb.shape[1]
    dff = conv_ffn_w.shape[2]
    gate_col = rnn_w // LANES
    q_off = 2 * rnn_w
    att_blk = ATT_HEADS * dh
    q_col = q_off // att_blk
    k_col = (q_off + sb_w) // att_blk
    v_col = (q_off + 2 * sb_w) // att_blk
    gm_col = (q_off + 3 * sb_w) // (d // 2)
    assert dh == LANES and q_off % att_blk == 0 and sb_w % att_blk == 0
    assert (q_off + 3 * sb_w) % (d // 2) == 0
    uu = _cumsum_matrix()

    x2 = x.reshape(bsz * seq, d)
    for l in range(depth):
        mod3 = _ada(c, w_ada[l], b_ada[l]).reshape(bsz, 6, d)
        p = _inproj(x2, mod3, g_norm1[l].reshape(1, d), w_in[l].astype(BF16), seq)

        wg = (0.5 * jnp.concatenate([_block_diag_pairs(w_rg_a[l]), _block_diag_pairs(w_rg_x[l])],
                                    axis=2)).astype(BF16)
        ya = _rglru(p, conv_rnn_w[l], conv_rnn_b[l].reshape(1, rnn_w), wg,
                    b_rg_a[l].reshape(1, rnn_w), b_rg_x[l].reshape(1, rnn_w),
                    lru_lambda[l].reshape(1, rnn_w), bsz, seq, 0, gate_col)
        ob = _attn(p, g_q[l].reshape(1, dh), g_k[l].reshape(1, dh), uu, bsz, seq,
                   sb_w // dh, dh, q_col, k_col, v_col)
        x1, h2 = _merge(ya, ob, p, x2, mod3, g_norm2[l].reshape(1, d),
                        w_proj_rnn[l].astype(BF16), w_proj_sb[l].astype(BF16),
                        w_out[l].astype(BF16), seq, gm_col)
        x2 = _ffn(h2, x1, mod3, w_up[l].astype(BF16), conv_ffn_w[l],
                  conv_ffn_b[l].reshape(1, dff), w_down[l].astype(BF16), seq)
    return x2.reshape(bsz, seq, d)
```

```python
import functools

import jax
import jax.numpy as jnp
from jax import lax
from jax.experimental import pallas as pl
from jax.experimental.pallas import tpu as pltpu

F32 = jnp.float32
BF16 = jnp.bfloat16

EPS = 1e-6
C_RG = 8.0
N_RNN_HEADS = 16
N_SB_HEADS = 8
RNN_CONV_W = 4
FFN_CONV_W = 3

LANES = 128
SUBLANES = 8
VMEM_LIMIT = 56 * 1024 * 1024

SCAN_FAN = 8
ATT_TQ = 512
ATT_TK = 256
ATT_HEADS = 2
LOG2_E = 1.4426950408889634


def _params(sem, vmem=VMEM_LIMIT, flags=None):
    return pltpu.CompilerParams(dimension_semantics=sem, vmem_limit_bytes=vmem, flags=flags)


def _ada_kernel(c_ref, w_ref, b_ref, o_ref):
    bsz = c_ref.shape[0]
    a = jax.nn.silu(c_ref[...])
    a_hi = a.astype(BF16)
    a_lo = (a - a_hi.astype(F32)).astype(BF16)
    w = w_ref[...]
    w_hi = w.astype(BF16)
    w_lo = (w - w_hi.astype(F32)).astype(BF16)
    r = jnp.dot(jnp.concatenate([a_hi, a_lo], axis=0), w_hi, preferred_element_type=F32)
    o_ref[...] = (r[:bsz] + r[bsz:] + jnp.dot(a_hi, w_lo, preferred_element_type=F32)
                  + b_ref[...])


def _ada(c, w, b, tn=1024):
    bsz, d = c.shape
    n = w.shape[1]
    return pl.pallas_call(
        _ada_kernel,
        grid=(n // tn,),
        in_specs=[pl.BlockSpec((bsz, d), lambda j: (0, 0)),
                  pl.BlockSpec((d, tn), lambda j: (0, j)),
                  pl.BlockSpec((1, tn), lambda j: (0, j))],
        out_specs=pl.BlockSpec((bsz, tn), lambda j: (0, j)),
        out_shape=jax.ShapeDtypeStruct((bsz, n), F32),
        compiler_params=_params(("arbitrary",)),
        name="ada",
    )(c, w, b.reshape(1, n))


def _norm_rows(x_ref, h_ref, g, scale, shift, r0, rows, chunk=128):
    for c0 in range(r0, r0 + rows, chunk):
        x = x_ref[c0:c0 + chunk, :]
        ms = jnp.mean(x * x, axis=-1, keepdims=True)
        y = x * lax.rsqrt(ms + EPS) * g
        h_ref[c0:c0 + chunk, :] = (y * (1.0 + scale) + shift).astype(h_ref.dtype)


def _inproj_kernel(x_ref, mod_ref, g_ref, w_ref, o_ref, h_s, *, rows):
    tm = x_ref.shape[0]

    def project(r0, nrows):
        o_ref[r0:r0 + nrows, :] = jnp.dot(h_s[r0:r0 + nrows, :], w_ref[...],
                                          preferred_element_type=F32).astype(o_ref.dtype)

    @pl.when(pl.program_id(1) == 0)
    def _():
        for r0 in range(0, tm, rows):
            _norm_rows(x_ref, h_s, g_ref[...], mod_ref[0, 1:2, :], mod_ref[0, 0:1, :], r0, rows)
            project(r0, rows)

    @pl.when(pl.program_id(1) != 0)
    def _():
        project(0, tm)


def _inproj(x2, mod3, g1, w, seq, tm=1024, tn=1536, rows=256):
    t, d = x2.shape
    n = w.shape[1]
    tpb = seq // tm
    return pl.pallas_call(
        functools.partial(_inproj_kernel, rows=rows),
        grid=(t // tm, n // tn),
        in_specs=[pl.BlockSpec((tm, d), lambda i, j: (i, 0)),
                  pl.BlockSpec((1,) + mod3.shape[1:], lambda i, j: (i // tpb, 0, 0)),
                  pl.BlockSpec((1, d), lambda i, j: (0, 0)),
                  pl.BlockSpec((d, tn), lambda i, j: (0, j))],
        out_specs=pl.BlockSpec((tm, tn), lambda i, j: (i, j)),
        out_shape=jax.ShapeDtypeStruct((t, n), BF16),
        scratch_shapes=[pltpu.VMEM((tm, d), BF16)],
        compiler_params=_params(("arbitrary", "arbitrary")),
        name="inproj",
    )(x2, mod3, g1, w)


def _scan_up(a_ref, b_ref, n, p_ref, l_ref, a2_ref, b2_ref, grp):
    m = n // SCAN_FAN
    for g0 in range(0, m, grp):
        p = l = None
        for k in range(SCAN_FAN):
            ak = a_ref[pl.ds(g0 * SCAN_FAN + k, grp, stride=SCAN_FAN), :]
            bk = b_ref[pl.ds(g0 * SCAN_FAN + k, grp, stride=SCAN_FAN), :]
            if k == 0:
                p, l = ak, bk
            else:
                l = ak * l + bk
                p = ak * p
            p_ref[k, g0:g0 + grp, :] = p
            l_ref[k, g0:g0 + grp, :] = l
        a2_ref[g0:g0 + grp, :] = p
        b2_ref[g0:g0 + grp, :] = l


def _scan_down(p_ref, l_ref, hin_ref, m, out_ref, grp, inclusive):
    for g0 in range(0, m, grp):
        e = hin_ref[g0:g0 + grp, :]
        for k in range(SCAN_FAN):
            if inclusive:
                v = l_ref[k, g0:g0 + grp, :] + p_ref[k, g0:g0 + grp, :] * e
            elif k == 0:
                v = e
            else:
                v = l_ref[k - 1, g0:g0 + grp, :] + p_ref[k - 1, g0:g0 + grp, :] * e
            out_ref[pl.ds(g0 * SCAN_FAN + k, grp, stride=SCAN_FAN), :] = v


def _rglru_kernel(x_ref, gt_ref, cw_ref, cb_ref, wg_ref, ba_ref, bx_ref, lam_ref, o_ref,
                  xbuf, a_s, b_s, p1, l1, a2, b2, hin1, p2, l2, a3, b3, e3, *, chunk):
    seq = x_ref.shape[0]
    n2 = seq // SCAN_FAN
    n3 = n2 // SCAN_FAN

    xbuf[0:SUBLANES, :] = jnp.zeros((SUBLANES, LANES), F32)
    for r0 in range(0, seq, chunk):
        xbuf[SUBLANES + r0:SUBLANES + r0 + chunk, :] = x_ref[r0:r0 + chunk, :].astype(F32)

    lam = lam_ref[...]
    neg = -lam
    softplus_neg_lam = jnp.maximum(neg, 0.0) + jnp.log1p(jnp.exp(-jnp.abs(neg)))
    half_coef = 0.5 * C_RG * softplus_neg_lam
    half_ba = 0.5 * ba_ref[...]
    half_bx = 0.5 * bx_ref[...]

    for r0 in range(0, seq, chunk):
        xr = cb_ref[...]
        for j in range(RNN_CONV_W):
            off = SUBLANES + r0 - (RNN_CONV_W - 1 - j)
            xr = xr + cw_ref[j:j + 1, :] * xbuf[off:off + chunk, :]
        g = jnp.dot(xr.astype(BF16), wg_ref[0], preferred_element_type=F32)
        tr = jnp.tanh(g[:, :LANES] + half_ba)
        i = 0.5 + 0.5 * jnp.tanh(g[:, LANES:] + half_bx)
        y = half_coef + half_coef * tr
        a_s[r0:r0 + chunk, :] = jnp.exp2(y * -LOG2_E)
        t = jnp.tanh(y)
        b_s[r0:r0 + chunk, :] = lax.rsqrt(0.5 + 0.5 / t) * (i * xr)

    _scan_up(a_s, b_s, seq, p1, l1, a2, b2, grp=64)
    _scan_up(a2, b2, n2, p2, l2, a3, b3, grp=n3)
    h = jnp.zeros((1, LANES), F32)
    for r in range(n3):
        e3[r:r + 1, :] = h
        h = a3[r:r + 1, :] * h + b3[r:r + 1, :]
    _scan_down(p2, l2, e3, n3, hin1, grp=n3, inclusive=False)
    _scan_down(p1, l1, hin1, n2, b_s, grp=64, inclusive=True)

    for r0 in range(0, seq, chunk):
        gt = gt_ref[r0:r0 + chunk, :].astype(F32)
        o_ref[r0:r0 + chunk, :] = (jax.nn.gelu(gt) * b_s[r0:r0 + chunk, :]).astype(o_ref.dtype)


def _rglru(p, cw, cb, wg, ba, bx, lam, bsz, seq, x_col, gate_col, chunk=256):
    width = cw.shape[1]
    nblk = width // LANES
    n2 = seq // SCAN_FAN
    n3 = n2 // SCAN_FAN
    vec = lambda: pl.BlockSpec((1, LANES), lambda b, c: (0, c))
    return pl.pallas_call(
        functools.partial(_rglru_kernel, chunk=chunk),
        grid=(bsz, nblk),
        in_specs=[pl.BlockSpec((seq, LANES), lambda b, c: (b, x_col + c)),
                  pl.BlockSpec((seq, LANES), lambda b, c: (b, gate_col + c)),
                  pl.BlockSpec((RNN_CONV_W, LANES), lambda b, c: (0, c)),
                  vec(),
                  pl.BlockSpec((1, LANES, 2 * LANES), lambda b, c: (c, 0, 0)),
                  vec(), vec(), vec()],
        out_specs=pl.BlockSpec((seq, LANES), lambda b, c: (b, c)),
        out_shape=jax.ShapeDtypeStruct((bsz * seq, width), BF16),
        scratch_shapes=[pltpu.VMEM((seq + SUBLANES, LANES), F32),
                        pltpu.VMEM((seq, LANES), F32), pltpu.VMEM((seq, LANES), F32),
                        pltpu.VMEM((SCAN_FAN, n2, LANES), F32), pltpu.VMEM((SCAN_FAN, n2, LANES), F32),
                        pltpu.VMEM((n2, LANES), F32), pltpu.VMEM((n2, LANES), F32),
                        pltpu.VMEM((n2, LANES), F32),
                        pltpu.VMEM((SCAN_FAN, n3, LANES), F32), pltpu.VMEM((SCAN_FAN, n3, LANES), F32),
                        pltpu.VMEM((n3, LANES), F32), pltpu.VMEM((n3, LANES), F32),
                        pltpu.VMEM((n3, LANES), F32)],
        compiler_params=_params(("arbitrary", "arbitrary")),
        name="rglru",
    )(p, p, cw, cb, wg, ba, bx, lam)


def _attn_kernel(q_ref, k_ref, v_ref, gq_ref, gk_ref, uu_ref, o_ref, qn_s, kn_s, acc_s, run_s):
    seq = q_ref.shape[0]
    dh = gq_ref.shape[1]
    scale = dh ** -0.5 * LOG2_E

    def norm_body(r, _):
        r0 = pl.multiple_of(r * ATT_TQ, ATT_TQ)
        for h in range(ATT_HEADS):
            hs = slice(h * dh, (h + 1) * dh)
            q = q_ref[pl.ds(r0, ATT_TQ), hs].astype(F32)
            qn = q * lax.rsqrt(jnp.mean(q * q, axis=-1, keepdims=True) + EPS) * gq_ref[...]
            qn_s[h, pl.ds(r0, ATT_TQ), :] = (qn * scale).astype(BF16)
            k = k_ref[pl.ds(r0, ATT_TQ), hs].astype(F32)
            kn = k * lax.rsqrt(jnp.mean(k * k, axis=-1, keepdims=True) + EPS) * gk_ref[...]
            kn_s[h, pl.ds(r0, ATT_TQ), :] = kn.astype(BF16)
        return 0
    lax.fori_loop(0, seq // ATT_TQ, norm_body, 0)

    nsub = ATT_TQ // ATT_TK

    heads = range(ATT_HEADS)

    def block(q0, k0, masked):
        zs = [lax.dot_general(qn_s[h, pl.ds(q0, ATT_TQ), :], kn_s[h, pl.ds(k0, ATT_TQ), :],
                              (((1,), (1,)), ((), ())), preferred_element_type=F32)
              for h in heads]
        if masked:
            tri = (lax.broadcasted_iota(jnp.int32, (ATT_TK, ATT_TK), 1)
                   < lax.broadcasted_iota(jnp.int32, (ATT_TK, ATT_TK), 0))

        def on_diag_rows(x, fill):
            top = jnp.where(tri, x[:ATT_TK], fill)
            return top if x.shape[0] == ATT_TK else jnp.concatenate([top, x[ATT_TK:]], axis=0)

        def neg_log_keep(z):
            neg_abs = pltpu.bitcast(pltpu.bitcast(z, jnp.uint32) | jnp.uint32(0x80000000), F32)
            return jnp.maximum(z, 0.0) + jnp.log2(1.0 + jnp.exp2(neg_abs))

        runs = [run_s[h] for h in heads]
        ws = [[None] * nsub for _ in heads]
        if not masked:
            ss = [neg_log_keep(zs[h]) for h in heads]
        for j in reversed(range(nsub)):
            sl = slice(j * ATT_TK, (j + 1) * ATT_TK)
            r0 = j * ATT_TK if masked else 0
            for h in heads:
                z = zs[h][r0:, sl]
                s = on_diag_rows(neg_log_keep(z), 0.0) if masked else ss[h][:, sl]
                cs = jnp.dot(s.astype(BF16), uu_ref[...], preferred_element_type=F32)
                run = runs[h][r0:]
                w = jnp.exp2(z + cs + jnp.concatenate([run] * (ATT_TK // LANES), axis=1))
                if masked:
                    w = on_diag_rows(w, 0.0)
                w = w.astype(BF16)
                run = run - jnp.sum(s, axis=-1, keepdims=True)
                if r0:
                    w = jnp.concatenate([jnp.zeros((r0, ATT_TK), BF16), w], axis=0)
                    run = jnp.concatenate([runs[h][:r0], run], axis=0)
                ws[h][j] = w
                runs[h] = run
        for h in heads:
            run_s[h] = runs[h]
            vblk = v_ref[pl.ds(k0, ATT_TQ), h * dh:(h + 1) * dh]
            acc_s[h] += jnp.dot(jnp.concatenate(ws[h], axis=1), vblk, preferred_element_type=F32)

    def q_body(qi, _):
        q0 = pl.multiple_of(qi * ATT_TQ, ATT_TQ)
        acc_s[...] = jnp.zeros_like(acc_s)
        run_s[...] = jnp.zeros_like(run_s)
        block(q0, q0, True)

        def k_body(j, _):
            block(q0, pl.multiple_of(q0 - (j + 1) * ATT_TQ, ATT_TQ), False)
            return 0
        lax.fori_loop(0, qi, k_body, 0)
        for h in range(ATT_HEADS):
            o_ref[pl.ds(q0, ATT_TQ), h * dh:(h + 1) * dh] = acc_s[h].astype(o_ref.dtype)
        return 0
    lax.fori_loop(0, seq // ATT_TQ, q_body, 0)


def _attn(p, gq, gk, uu, bsz, seq, n_heads, dh, q_col, k_col, v_col):
    wblk = ATT_HEADS * dh
    return pl.pallas_call(
        _attn_kernel,
        grid=(bsz, n_heads // ATT_HEADS),
        in_specs=[pl.BlockSpec((seq, wblk), lambda b, h: (b, q_col + h)),
                  pl.BlockSpec((seq, wblk), lambda b, h: (b, k_col + h)),
                  pl.BlockSpec((seq, wblk), lambda b, h: (b, v_col + h)),
                  pl.BlockSpec((1, dh), lambda b, h: (0, 0)),
                  pl.BlockSpec((1, dh), lambda b, h: (0, 0)),
                  pl.BlockSpec(uu.shape, lambda b, h: (0, 0))],
        out_specs=pl.BlockSpec((seq, wblk), lambda b, h: (b, h)),
        out_shape=jax.ShapeDtypeStruct((bsz * seq, n_heads * dh), BF16),
        scratch_shapes=[pltpu.VMEM((ATT_HEADS, seq, dh), BF16),
                        pltpu.VMEM((ATT_HEADS, seq, dh), BF16),
                        pltpu.VMEM((ATT_HEADS, ATT_TQ, dh), F32),
                        pltpu.VMEM((ATT_HEADS, ATT_TQ, LANES), F32)],
        compiler_params=_params(("arbitrary", "arbitrary")),
        name="attn",
    )(p, p, p, gq, gk, uu)


def _merge_kernel(ya_ref, ob_ref, gr0_ref, gr1_ref, gs0_ref, gs1_ref, x_ref, mod_ref, g2_ref,
                  wr_ref, wsb_ref, wo_ref, x1_ref, h2_ref, mg_s, *, nc, rows):
    tm, d = x_ref.shape
    half = d // 2
    for n0 in range(0, d, nc):
        yr = jnp.dot(ya_ref[...], wr_ref[:, n0:n0 + nc], preferred_element_type=F32)
        ys = jnp.dot(ob_ref[...], wsb_ref[:, n0:n0 + nc], preferred_element_type=F32)
        gr_ref, gs_ref = (gr0_ref, gs0_ref) if n0 < half else (gr1_ref, gs1_ref)
        m0 = n0 % half
        gr = jax.nn.sigmoid(gr_ref[:, m0:m0 + nc].astype(F32))
        gs = jax.nn.sigmoid(gs_ref[:, m0:m0 + nc].astype(F32))
        mg_s[:, n0:n0 + nc] = (gr * yr + gs * ys).astype(BF16)
    for r0 in range(0, tm, rows):
        rs = slice(r0, r0 + rows)
        for n0 in range(0, d, nc):
            y = jnp.dot(mg_s[rs, :], wo_ref[:, n0:n0 + nc], preferred_element_type=F32)
            x1_ref[rs, n0:n0 + nc] = x_ref[rs, n0:n0 + nc] + mod_ref[0, 2:3, n0:n0 + nc] * y
        _norm_rows(x1_ref, h2_ref, g2_ref[...], mod_ref[0, 4:5, :], mod_ref[0, 3:4, :], r0, rows)


def _merge(ya, ob, p, x2, mod3, g2, wr, wsb, wo, seq, gm_col, tm=512, nc=512, rows=256):
    t, d = x2.shape
    w = ya.shape[1]
    half = d // 2
    tpb = seq // tm
    const = lambda shape: pl.BlockSpec(shape, lambda i: (0, 0), pipeline_mode=pl.Buffered(1))
    gate = lambda k: pl.BlockSpec((tm, half), lambda i: (i, gm_col + k))
    return pl.pallas_call(
        functools.partial(_merge_kernel, nc=nc, rows=rows),
        grid=(t // tm,),
        in_specs=[pl.BlockSpec((tm, w), lambda i: (i, 0)),
                  pl.BlockSpec((tm, w), lambda i: (i, 0)),
                  gate(0), gate(1), gate(2), gate(3),
                  pl.BlockSpec((tm, d), lambda i: (i, 0)),
                  pl.BlockSpec((1,) + mod3.shape[1:], lambda i: (i // tpb, 0, 0)),
                  pl.BlockSpec((1, d), lambda i: (0, 0)),
                  const(wr.shape), const(wsb.shape), const(wo.shape)],
        out_specs=[pl.BlockSpec((tm, d), lambda i: (i, 0)),
                   pl.BlockSpec((tm, d), lambda i: (i, 0))],
        out_shape=[jax.ShapeDtypeStruct((t, d), F32), jax.ShapeDtypeStruct((t, d), BF16)],
        scratch_shapes=[pltpu.VMEM((tm, d), BF16)],
        compiler_params=_params(("arbitrary",)),
        name="merge",
    )(ya, ob, p, p, p, p, x2, mod3, g2, wr, wsb, wo)


def _ffn_kernel(h_ref, x1_ref, mod_ref, wv_ref, wg_ref, cw_ref, cb_ref, wd_ref, o_ref,
                ubuf, halo, act_s, val_s, *, tiles_per_seq, sub, nsub):
    i = pl.program_id(0)
    f = pl.program_id(1)
    tm = h_ref.shape[0]
    tf = wv_ref.shape[1]
    d = o_ref.shape[1]

    @pl.when(f == 0)
    def _():
        o_ref[...] = jnp.zeros_like(o_ref)

    @pl.when(i % tiles_per_seq == 0)
    def _():
        ubuf[0:SUBLANES, :] = jnp.zeros((SUBLANES, tf), F32)

    @pl.when(i % tiles_per_seq != 0)
    def _():
        ubuf[0:SUBLANES, :] = halo[f]

    h = h_ref[...]

    def up(c0):
        cs = slice(c0, c0 + sub)
        val_s[:, cs] = jnp.dot(h, wv_ref[:, cs], preferred_element_type=F32)
        ubuf[SUBLANES:, cs] = jnp.dot(h, wg_ref[:, cs], preferred_element_type=F32)

    def gate(c0):
        cs = slice(c0, c0 + sub)
        g = cb_ref[:, cs]
        for j in range(FFN_CONV_W):
            off = SUBLANES - (FFN_CONV_W - 1 - j)
            g = g + cw_ref[j:j + 1, cs] * ubuf[off:off + tm, cs]
        act_s[:, cs] = (jax.nn.gelu(g) * val_s[:, cs]).astype(BF16)

    def down(k0):
        ks = slice(k0, k0 + tf // 2)
        for n0 in range(0, d, nsub):
            o_ref[:, n0:n0 + nsub] += jnp.dot(act_s[:, ks], wd_ref[ks, n0:n0 + nsub],
                                              preferred_element_type=F32)

    up(0)
    for c0 in range(sub, tf, sub):
        up(c0)
        gate(c0 - sub)
    assert (tf // 2) % sub == 0 and tf // 2 <= tf - sub
    down(0)
    gate(tf - sub)
    halo[f] = ubuf[tm:tm + SUBLANES, :]
    down(tf // 2)

    @pl.when(f == pl.num_programs(1) - 1)
    def _():
        o_ref[...] = x1_ref[...] + mod_ref[0, 5:6, :] * o_ref[...]


def _ffn(h2, x1, mod3, w_up, cw, cb, wd, seq, tm=512, tf=1024, sub=256, nsub=512):
    t, d = x1.shape
    dff = w_up.shape[1] // 2
    tpb = seq // tm
    nf = dff // tf
    return pl.pallas_call(
        functools.partial(_ffn_kernel, tiles_per_seq=tpb, sub=sub, nsub=nsub),
        grid=(t // tm, nf),
        in_specs=[pl.BlockSpec((tm, d), lambda i, f: (i, 0)),
                  pl.BlockSpec((tm, d), lambda i, f: (i, 0)),
                  pl.BlockSpec((1,) + mod3.shape[1:], lambda i, f: (i // tpb, 0, 0)),
                  pl.BlockSpec((d, tf), lambda i, f: (0, f)),
                  pl.BlockSpec((d, tf), lambda i, f: (0, nf + f)),
                  pl.BlockSpec((FFN_CONV_W, tf), lambda i, f: (0, f)),
                  pl.BlockSpec((1, tf), lambda i, f: (0, f)),
                  pl.BlockSpec((tf, d), lambda i, f: (f, 0))],
        out_specs=pl.BlockSpec((tm, d), lambda i, f: (i, 0)),
        out_shape=jax.ShapeDtypeStruct((t, d), F32),
        scratch_shapes=[pltpu.VMEM((tm + SUBLANES, tf), F32),
                        pltpu.VMEM((nf, SUBLANES, tf), F32),
                        pltpu.VMEM((tm, tf), BF16),
                        pltpu.VMEM((tm, tf), F32)],
        compiler_params=_params(("arbitrary", "arbitrary")),
        name="ffn",
    )(h2, x1, mod3, w_up, w_up, cw, cb, wd)


def _block_diag_pairs(w):
    nh, hd, _ = w.shape
    w = w.reshape(nh // 2, 2, hd, hd)
    z = jnp.zeros_like(w[:, 0])
    top = jnp.concatenate([w[:, 0], z], axis=2)
    bot = jnp.concatenate([z, w[:, 1]], axis=2)
    return jnp.concatenate([top, bot], axis=1)


def _cumsum_matrix():
    j = jnp.arange(ATT_TK)[:, None]
    s = jnp.arange(ATT_TK)[None, :]
    return -(j >= s).astype(BF16)


def kernel(x, c, w_ada, b_ada, g_norm1, w_in, conv_rnn_w, conv_rnn_b, w_rg_a, b_rg_a, w_rg_x,
           b_rg_x, lru_lambda, g_q, g_k, w_proj_rnn, w_proj_sb, w_out, g_norm2, w_up,
           conv_ffn_w, conv_ffn_b, w_down):
    bsz, seq, d = x.shape
    depth = w_ada.shape[0]
    rnn_w = conv_rnn_w.shape[2]
    dh = g_q.shape[1]
    sb_w = w_proj_sb.shape[1]
    dff = conv_ffn_w.shape[2]
    gate_col = rnn_w // LANES
    q_off = 2 * rnn_w
    att_blk = ATT_HEADS * dh
    q_col = q_off // att_blk
    k_col = (q_off + sb_w) // att_blk
    v_col = (q_off + 2 * sb_w) // att_blk
    gm_col = (q_off + 3 * sb_w) // (d // 2)
    assert dh == LANES and q_off % att_blk == 0 and sb_w % att_blk == 0
    assert (q_off + 3 * sb_w) % (d // 2) == 0
    uu = _cumsum_matrix()

    x2 = x.reshape(bsz * seq, d)
    for l in range(depth):
        mod3 = _ada(c, w_ada[l], b_ada[l]).reshape(bsz, 6, d)
        p = _inproj(x2, mod3, g_norm1[l].reshape(1, d), w_in[l].astype(BF16), seq)

        wg = (0.5 * jnp.concatenate([_block_diag_pairs(w_rg_a[l]), _block_diag_pairs(w_rg_x[l])],
                                    axis=2)).astype(BF16)
        ya = _rglru(p, conv_rnn_w[l], conv_rnn_b[l].reshape(1, rnn_w), wg,
                    b_rg_a[l].reshape(1, rnn_w), b_rg_x[l].reshape(1, rnn_w),
                    lru_lambda[l].reshape(1, rnn_w), bsz, seq, 0, gate_col)
        ob = _attn(p, g_q[l].reshape(1, dh), g_k[l].reshape(1, dh), uu, bsz, seq,
                   sb_w // dh, dh, q_col, k_col, v_col)
        x1, h2 = _merge(ya, ob, p, x2, mod3, g_norm2[l].reshape(1, d),
                        w_proj_rnn[l].astype(BF16), w_proj_sb[l].astype(BF16),
                        w_out[l].astype(BF16), seq, gm_col)
        x2 = _ffn(h2, x1, mod3, w_up[l].astype(BF16), conv_ffn_w[l],
                  conv_ffn_b[l].reshape(1, dff), w_down[l].astype(BF16), seq)
    return x2.reshape(bsz, seq, d)
```

```python
import functools

import jax
import jax.numpy as jnp
from jax import lax
from jax.experimental import pallas as pl
from jax.experimental.pallas import tpu as pltpu

F32 = jnp.float32
BF16 = jnp.bfloat16

EPS = 1e-6
C_RG = 8.0
N_RNN_HEADS = 16
N_SB_HEADS = 8
RNN_CONV_W = 4
FFN_CONV_W = 3

LANES = 128
SUBLANES = 8
VMEM_LIMIT = 56 * 1024 * 1024

SCAN_FAN = 8
ATT_TQ = 512
ATT_TK = 256
ATT_HEADS = 4
LOG2_E = 1.4426950408889634


def _params(sem, vmem=VMEM_LIMIT, flags=None):
    return pltpu.CompilerParams(dimension_semantics=sem, vmem_limit_bytes=vmem, flags=flags)


def _ada_kernel(c_ref, w_ref, b_ref, o_ref):
    bsz = c_ref.shape[0]
    a = jax.nn.silu(c_ref[...])
    a_hi = a.astype(BF16)
    a_lo = (a - a_hi.astype(F32)).astype(BF16)
    w = w_ref[...]
    w_hi = w.astype(BF16)
    w_lo = (w - w_hi.astype(F32)).astype(BF16)
    r = jnp.dot(jnp.concatenate([a_hi, a_lo], axis=0), w_hi, preferred_element_type=F32)
    o_ref[...] = (r[:bsz] + r[bsz:] + jnp.dot(a_hi, w_lo, preferred_element_type=F32)
                  + b_ref[...])


def _ada(c, w, b, tn=1024):
    bsz, d = c.shape
    n = w.shape[1]
    return pl.pallas_call(
        _ada_kernel,
        grid=(n // tn,),
        in_specs=[pl.BlockSpec((bsz, d), lambda j: (0, 0)),
                  pl.BlockSpec((d, tn), lambda j: (0, j)),
                  pl.BlockSpec((1, tn), lambda j: (0, j))],
        out_specs=pl.BlockSpec((bsz, tn), lambda j: (0, j)),
        out_shape=jax.ShapeDtypeStruct((bsz, n), F32),
        compiler_params=_params(("arbitrary",)),
        name="ada",
    )(c, w, b.reshape(1, n))


def _norm_rows(x_ref, h_ref, g, scale, shift, r0, rows, chunk=128):
    for c0 in range(r0, r0 + rows, chunk):
        x = x_ref[c0:c0 + chunk, :]
        ms = jnp.mean(x * x, axis=-1, keepdims=True)
        y = x * lax.rsqrt(ms + EPS) * g
        h_ref[c0:c0 + chunk, :] = (y * (1.0 + scale) + shift).astype(h_ref.dtype)


def _inproj_kernel(x_ref, mod_ref, g_ref, w_ref, o_ref, h_s, *, rows):
    tm = x_ref.shape[0]

    def project(r0, nrows):
        o_ref[r0:r0 + nrows, :] = jnp.dot(h_s[r0:r0 + nrows, :], w_ref[...],
                                          preferred_element_type=F32).astype(o_ref.dtype)

    @pl.when(pl.program_id(1) == 0)
    def _():
        for r0 in range(0, tm, rows):
            _norm_rows(x_ref, h_s, g_ref[...], mod_ref[0, 1:2, :], mod_ref[0, 0:1, :], r0, rows)
            project(r0, rows)

    @pl.when(pl.program_id(1) != 0)
    def _():
        project(0, tm)


def _inproj(x2, mod3, g1, w, seq, tm=1024, tn=1536, rows=256):
    t, d = x2.shape
    n = w.shape[1]
    tpb = seq // tm
    return pl.pallas_call(
        functools.partial(_inproj_kernel, rows=rows),
        grid=(t // tm, n // tn),
        in_specs=[pl.BlockSpec((tm, d), lambda i, j: (i, 0)),
                  pl.BlockSpec((1,) + mod3.shape[1:], lambda i, j: (i // tpb, 0, 0)),
                  pl.BlockSpec((1, d), lambda i, j: (0, 0)),
                  pl.BlockSpec((d, tn), lambda i, j: (0, j))],
        out_specs=pl.BlockSpec((tm, tn), lambda i, j: (i, j)),
        out_shape=jax.ShapeDtypeStruct((t, n), BF16),
        scratch_shapes=[pltpu.VMEM((tm, d), BF16)],
        compiler_params=_params(("arbitrary", "arbitrary")),
        name="inproj",
    )(x2, mod3, g1, w)


def _scan_up(a_ref, b_ref, n, p_ref, l_ref, a2_ref, b2_ref, grp):
    m = n // SCAN_FAN
    for g0 in range(0, m, grp):
        p = l = None
        for k in range(SCAN_FAN):
            ak = a_ref[pl.ds(g0 * SCAN_FAN + k, grp, stride=SCAN_FAN), :]
            bk = b_ref[pl.ds(g0 * SCAN_FAN + k, grp, stride=SCAN_FAN), :]
            if k == 0:
                p, l = ak, bk
            else:
                l = ak * l + bk
                p = ak * p
            p_ref[k, g0:g0 + grp, :] = p
            l_ref[k, g0:g0 + grp, :] = l
        a2_ref[g0:g0 + grp, :] = p
        b2_ref[g0:g0 + grp, :] = l


def _scan_down(p_ref, l_ref, hin_ref, m, out_ref, grp, inclusive):
    for g0 in range(0, m, grp):
        e = hin_ref[g0:g0 + grp, :]
        for k in range(SCAN_FAN):
            if inclusive:
                v = l_ref[k, g0:g0 + grp, :] + p_ref[k, g0:g0 + grp, :] * e
            elif k == 0:
                v = e
            else:
                v = l_ref[k - 1, g0:g0 + grp, :] + p_ref[k - 1, g0:g0 + grp, :] * e
            out_ref[pl.ds(g0 * SCAN_FAN + k, grp, stride=SCAN_FAN), :] = v


def _rglru_kernel(x_ref, gt_ref, cw_ref, cb_ref, wg_ref, ba_ref, bx_ref, lam_ref, o_ref,
                  xbuf, a_s, b_s, p1, l1, a2, b2, hin1, p2, l2, a3, b3, e3, *, chunk):
    seq = x_ref.shape[0]
    n2 = seq // SCAN_FAN
    n3 = n2 // SCAN_FAN

    xbuf[0:SUBLANES, :] = jnp.zeros((SUBLANES, LANES), F32)
    for r0 in range(0, seq, chunk):
        xbuf[SUBLANES + r0:SUBLANES + r0 + chunk, :] = x_ref[r0:r0 + chunk, :].astype(F32)

    lam = lam_ref[...]
    neg = -lam
    softplus_neg_lam = jnp.maximum(neg, 0.0) + jnp.log1p(jnp.exp(-jnp.abs(neg)))
    half_coef = 0.5 * C_RG * softplus_neg_lam
    half_ba = 0.5 * ba_ref[...]
    half_bx = 0.5 * bx_ref[...]

    for r0 in range(0, seq, chunk):
        xr = cb_ref[...]
        for j in range(RNN_CONV_W):
            off = SUBLANES + r0 - (RNN_CONV_W - 1 - j)
            xr = xr + cw_ref[j:j + 1, :] * xbuf[off:off + chunk, :]
        g = jnp.dot(xr.astype(BF16), wg_ref[0], preferred_element_type=F32)
        tr = jnp.tanh(g[:, :LANES] + half_ba)
        i = 0.5 + 0.5 * jnp.tanh(g[:, LANES:] + half_bx)
        y = half_coef + half_coef * tr
        a_s[r0:r0 + chunk, :] = jnp.exp2(y * -LOG2_E)
        t = jnp.tanh(y)
        b_s[r0:r0 + chunk, :] = lax.rsqrt(0.5 + 0.5 / t) * (i * xr)

    _scan_up(a_s, b_s, seq, p1, l1, a2, b2, grp=64)
    _scan_up(a2, b2, n2, p2, l2, a3, b3, grp=n3)
    h = jnp.zeros((1, LANES), F32)
    for r in range(n3):
        e3[r:r + 1, :] = h
        h = a3[r:r + 1, :] * h + b3[r:r + 1, :]
    _scan_down(p2, l2, e3, n3, hin1, grp=n3, inclusive=False)
    _scan_down(p1, l1, hin1, n2, b_s, grp=64, inclusive=True)

    for r0 in range(0, seq, chunk):
        gt = gt_ref[r0:r0 + chunk, :].astype(F32)
        o_ref[r0:r0 + chunk, :] = (jax.nn.gelu(gt) * b_s[r0:r0 + chunk, :]).astype(o_ref.dtype)


def _rglru(p, cw, cb, wg, ba, bx, lam, bsz, seq, x_col, gate_col, chunk=256):
    width = cw.shape[1]
    nblk = width // LANES
    n2 = seq // SCAN_FAN
    n3 = n2 // SCAN_FAN
    vec = lambda: pl.BlockSpec((1, LANES), lambda b, c: (0, c))
    return pl.pallas_call(
        functools.partial(_rglru_kernel, chunk=chunk),
        grid=(bsz, nblk),
        in_specs=[pl.BlockSpec((seq, LANES), lambda b, c: (b, x_col + c)),
                  pl.BlockSpec((seq, LANES), lambda b, c: (b, gate_col + c)),
                  pl.BlockSpec((RNN_CONV_W, LANES), lambda b, c: (0, c)),
                  vec(),
                  pl.BlockSpec((1, LANES, 2 * LANES), lambda b, c: (c, 0, 0)),
                  vec(), vec(), vec()],
        out_specs=pl.BlockSpec((seq, LANES), lambda b, c: (b, c)),
        out_shape=jax.ShapeDtypeStruct((bsz * seq, width), BF16),
        scratch_shapes=[pltpu.VMEM((seq + SUBLANES, LANES), F32),
                        pltpu.VMEM((seq, LANES), F32), pltpu.VMEM((seq, LANES), F32),
                        pltpu.VMEM((SCAN_FAN, n2, LANES), F32), pltpu.VMEM((SCAN_FAN, n2, LANES), F32),
                        pltpu.VMEM((n2, LANES), F32), pltpu.VMEM((n2, LANES), F32),
                        pltpu.VMEM((n2, LANES), F32),
                        pltpu.VMEM((SCAN_FAN, n3, LANES), F32), pltpu.VMEM((SCAN_FAN, n3, LANES), F32),
                        pltpu.VMEM((n3, LANES), F32), pltpu.VMEM((n3, LANES), F32),
                        pltpu.VMEM((n3, LANES), F32)],
        compiler_params=_params(("arbitrary", "arbitrary")),
        name="rglru",
    )(p, p, cw, cb, wg, ba, bx, lam)


def _attn_kernel(q_ref, k_ref, v_ref, gq_ref, gk_ref, uu_ref, o_ref, qn_s, kn_s, acc_s, run_s):
    seq = q_ref.shape[0]
    dh = gq_ref.shape[1]
    scale = dh ** -0.5 * LOG2_E

    def norm_body(r, _):
        r0 = pl.multiple_of(r * ATT_TQ, ATT_TQ)
        for h in range(ATT_HEADS):
            hs = slice(h * dh, (h + 1) * dh)
            q = q_ref[pl.ds(r0, ATT_TQ), hs].astype(F32)
            qn = q * lax.rsqrt(jnp.mean(q * q, axis=-1, keepdims=True) + EPS) * gq_ref[...]
            qn_s[h, pl.ds(r0, ATT_TQ), :] = (qn * scale).astype(BF16)
            k = k_ref[pl.ds(r0, ATT_TQ), hs].astype(F32)
            kn = k * lax.rsqrt(jnp.mean(k * k, axis=-1, keepdims=True) + EPS) * gk_ref[...]
            kn_s[h, pl.ds(r0, ATT_TQ), :] = kn.astype(BF16)
        return 0
    lax.fori_loop(0, seq // ATT_TQ, norm_body, 0)

    nsub = ATT_TQ // ATT_TK

    heads = range(ATT_HEADS)

    def block(q0, k0, masked):
        zs = [lax.dot_general(qn_s[h, pl.ds(q0, ATT_TQ), :], kn_s[h, pl.ds(k0, ATT_TQ), :],
                              (((1,), (1,)), ((), ())), preferred_element_type=F32)
              for h in heads]
        if masked:
            tri = (lax.broadcasted_iota(jnp.int32, (ATT_TK, ATT_TK), 1)
                   < lax.broadcasted_iota(jnp.int32, (ATT_TK, ATT_TK), 0))

        def on_diag_rows(x, fill):
            top = jnp.where(tri, x[:ATT_TK], fill)
            return top if x.shape[0] == ATT_TK else jnp.concatenate([top, x[ATT_TK:]], axis=0)

        def neg_log_keep(z):
            neg_abs = pltpu.bitcast(pltpu.bitcast(z, jnp.uint32) | jnp.uint32(0x80000000), F32)
            return jnp.maximum(z, 0.0) + jnp.log2(1.0 + jnp.exp2(neg_abs))

        runs = [run_s[h] for h in heads]
        ws = [[None] * nsub for _ in heads]
        if not masked:
            ss = [neg_log_keep(zs[h]) for h in heads]
        for j in reversed(range(nsub)):
            sl = slice(j * ATT_TK, (j + 1) * ATT_TK)
            r0 = j * ATT_TK if masked else 0
            for h in heads:
                z = zs[h][r0:, sl]
                s = on_diag_rows(neg_log_keep(z), 0.0) if masked else ss[h][:, sl]
                cs = jnp.dot(s.astype(BF16), uu_ref[...], preferred_element_type=F32)
                run = runs[h][r0:]
                w = jnp.exp2(z + cs + jnp.concatenate([run] * (ATT_TK // LANES), axis=1))
                if masked:
                    w = on_diag_rows(w, 0.0)
                w = w.astype(BF16)
                run = run - jnp.sum(s, axis=-1, keepdims=True)
                if r0:
                    w = jnp.concatenate([jnp.zeros((r0, ATT_TK), BF16), w], axis=0)
                    run = jnp.concatenate([runs[h][:r0], run], axis=0)
                ws[h][j] = w
                runs[h] = run
        for h in heads:
            run_s[h] = runs[h]
            vblk = v_ref[pl.ds(k0, ATT_TQ), h * dh:(h + 1) * dh]
            acc_s[h] += jnp.dot(jnp.concatenate(ws[h], axis=1), vblk, preferred_element_type=F32)

    def q_body(qi, _):
        q0 = pl.multiple_of(qi * ATT_TQ, ATT_TQ)
        acc_s[...] = jnp.zeros_like(acc_s)
        run_s[...] = jnp.zeros_like(run_s)
        block(q0, q0, True)

        def k_body(j, _):
            block(q0, pl.multiple_of(q0 - (j + 1) * ATT_TQ, ATT_TQ), False)
            return 0
        lax.fori_loop(0, qi, k_body, 0)
        for h in range(ATT_HEADS):
            o_ref[pl.ds(q0, ATT_TQ), h * dh:(h + 1) * dh] = acc_s[h].astype(o_ref.dtype)
        return 0
    lax.fori_loop(0, seq // ATT_TQ, q_body, 0)


def _attn(p, gq, gk, uu, bsz, seq, n_heads, dh, q_col, k_col, v_col):
    wblk = ATT_HEADS * dh
    return pl.pallas_call(
        _attn_kernel,
        grid=(bsz, n_heads // ATT_HEADS),
        in_specs=[pl.BlockSpec((seq, wblk), lambda b, h: (b, q_col + h)),
                  pl.BlockSpec((seq, wblk), lambda b, h: (b, k_col + h)),
                  pl.BlockSpec((seq, wblk), lambda b, h: (b, v_col + h)),
                  pl.BlockSpec((1, dh), lambda b, h: (0, 0)),
                  pl.BlockSpec((1, dh), lambda b, h: (0, 0)),
                  pl.BlockSpec(uu.shape, lambda b, h: (0, 0))],
        out_specs=pl.BlockSpec((seq, wblk), lambda b, h: (b, h)),
        out_shape=jax.ShapeDtypeStruct((bsz * seq, n_heads * dh), BF16),
        scratch_shapes=[pltpu.VMEM((ATT_HEADS, seq, dh), BF16),
                        pltpu.VMEM((ATT_HEADS, seq, dh), BF16),
                        pltpu.VMEM((ATT_HEADS, ATT_TQ, dh), F32),
                        pltpu.VMEM((ATT_HEADS, ATT_TQ, LANES), F32)],
        compiler_params=_params(("arbitrary", "arbitrary")),
        name="attn",
    )(p, p, p, gq, gk, uu)


def _merge_kernel(ya_ref, ob_ref, gr0_ref, gr1_ref, gs0_ref, gs1_ref, x_ref, mod_ref, g2_ref,
                  wr_ref, wsb_ref, wo_ref, x1_ref, h2_ref, mg_s, *, nc, rows):
    tm, d = x_ref.shape
    half = d // 2
    for n0 in range(0, d, nc):
        yr = jnp.dot(ya_ref[...], wr_ref[:, n0:n0 + nc], preferred_element_type=F32)
        ys = jnp.dot(ob_ref[...], wsb_ref[:, n0:n0 + nc], preferred_element_type=F32)
        gr_ref, gs_ref = (gr0_ref, gs0_ref) if n0 < half else (gr1_ref, gs1_ref)
        m0 = n0 % half
        gr = jax.nn.sigmoid(gr_ref[:, m0:m0 + nc].astype(F32))
        gs = jax.nn.sigmoid(gs_ref[:, m0:m0 + nc].astype(F32))
        mg_s[:, n0:n0 + nc] = (gr * yr + gs * ys).astype(BF16)
    for r0 in range(0, tm, rows):
        rs = slice(r0, r0 + rows)
        for n0 in range(0, d, nc):
            y = jnp.dot(mg_s[rs, :], wo_ref[:, n0:n0 + nc], preferred_element_type=F32)
            x1_ref[rs, n0:n0 + nc] = x_ref[rs, n0:n0 + nc] + mod_ref[0, 2:3, n0:n0 + nc] * y
        _norm_rows(x1_ref, h2_ref, g2_ref[...], mod_ref[0, 4:5, :], mod_ref[0, 3:4, :], r0, rows)


def _merge(ya, ob, p, x2, mod3, g2, wr, wsb, wo, seq, gm_col, tm=512, nc=512, rows=256):
    t, d = x2.shape
    w = ya.shape[1]
    half = d // 2
    tpb = seq // tm
    const = lambda shape: pl.BlockSpec(shape, lambda i: (0, 0), pipeline_mode=pl.Buffered(1))
    gate = lambda k: pl.BlockSpec((tm, half), lambda i: (i, gm_col + k))
    return pl.pallas_call(
        functools.partial(_merge_kernel, nc=nc, rows=rows),
        grid=(t // tm,),
        in_specs=[pl.BlockSpec((tm, w), lambda i: (i, 0)),
                  pl.BlockSpec((tm, w), lambda i: (i, 0)),
                  gate(0), gate(1), gate(2), gate(3),
                  pl.BlockSpec((tm, d), lambda i: (i, 0)),
                  pl.BlockSpec((1,) + mod3.shape[1:], lambda i: (i // tpb, 0, 0)),
                  pl.BlockSpec((1, d), lambda i: (0, 0)),
                  const(wr.shape), const(wsb.shape), const(wo.shape)],
        out_specs=[pl.BlockSpec((tm, d), lambda i: (i, 0)),
                   pl.BlockSpec((tm, d), lambda i: (i, 0))],
        out_shape=[jax.ShapeDtypeStruct((t, d), F32), jax.ShapeDtypeStruct((t, d), BF16)],
        scratch_shapes=[pltpu.VMEM((tm, d), BF16)],
        compiler_params=_params(("arbitrary",)),
        name="merge",
    )(ya, ob, p, p, p, p, x2, mod3, g2, wr, wsb, wo)


def _ffn_kernel(h_ref, x1_ref, mod_ref, wv_ref, wg_ref, cw_ref, cb_ref, wd_ref, o_ref,
                ubuf, halo, act_s, val_s, *, tiles_per_seq, sub, nsub):
    i = pl.program_id(0)
    f = pl.program_id(1)
    tm = h_ref.shape[0]
    tf = wv_ref.shape[1]
    d = o_ref.shape[1]

    @pl.when(f == 0)
    def _():
        o_ref[...] = jnp.zeros_like(o_ref)

    @pl.when(i % tiles_per_seq == 0)
    def _():
        ubuf[0:SUBLANES, :] = jnp.zeros((SUBLANES, tf), F32)

    @pl.when(i % tiles_per_seq != 0)
    def _():
        ubuf[0:SUBLANES, :] = halo[f]

    h = h_ref[...]

    def up(c0):
        cs = slice(c0, c0 + sub)
        val_s[:, cs] = jnp.dot(h, wv_ref[:, cs], preferred_element_type=F32)
        ubuf[SUBLANES:, cs] = jnp.dot(h, wg_ref[:, cs], preferred_element_type=F32)

    def gate(c0):
        cs = slice(c0, c0 + sub)
        g = cb_ref[:, cs]
        for j in range(FFN_CONV_W):
            off = SUBLANES - (FFN_CONV_W - 1 - j)
            g = g + cw_ref[j:j + 1, cs] * ubuf[off:off + tm, cs]
        act_s[:, cs] = (jax.nn.gelu(g) * val_s[:, cs]).astype(BF16)

    def down(k0):
        ks = slice(k0, k0 + tf // 2)
        for n0 in range(0, d, nsub):
            o_ref[:, n0:n0 + nsub] += jnp.dot(act_s[:, ks], wd_ref[ks, n0:n0 + nsub],
                                              preferred_element_type=F32)

    up(0)
    for c0 in range(sub, tf, sub):
        up(c0)
        gate(c0 - sub)
    assert (tf // 2) % sub == 0 and tf // 2 <= tf - sub
    down(0)
    gate(tf - sub)
    halo[f] = ubuf[tm:tm + SUBLANES, :]
    down(tf // 2)

    @pl.when(f == pl.num_programs(1) - 1)
    def _():
        o_ref[...] = x1_ref[...] + mod_ref[0, 5:6, :] * o_ref[...]


def _ffn(h2, x1, mod3, w_up, cw, cb, wd, seq, tm=512, tf=1024, sub=256, nsub=512):
    t, d = x1.shape
    dff = w_up.shape[1] // 2
    tpb = seq // tm
    nf = dff // tf
    return pl.pallas_call(
        functools.partial(_ffn_kernel, tiles_per_seq=tpb, sub=sub, nsub=nsub),
        grid=(t // tm, nf),
        in_specs=[pl.BlockSpec((tm, d), lambda i, f: (i, 0)),
                  pl.BlockSpec((tm, d), lambda i, f: (i, 0)),
                  pl.BlockSpec((1,) + mod3.shape[1:], lambda i, f: (i // tpb, 0, 0)),
                  pl.BlockSpec((d, tf), lambda i, f: (0, f)),
                  pl.BlockSpec((d, tf), lambda i, f: (0, nf + f)),
                  pl.BlockSpec((FFN_CONV_W, tf), lambda i, f: (0, f)),
                  pl.BlockSpec((1, tf), lambda i, f: (0, f)),
                  pl.BlockSpec((tf, d), lambda i, f: (f, 0))],
        out_specs=pl.BlockSpec((tm, d), lambda i, f: (i, 0)),
        out_shape=jax.ShapeDtypeStruct((t, d), F32),
        scratch_shapes=[pltpu.VMEM((tm + SUBLANES, tf), F32),
                        pltpu.VMEM((nf, SUBLANES, tf), F32),
                        pltpu.VMEM((tm, tf), BF16),
                        pltpu.VMEM((tm, tf), F32)],
        compiler_params=_params(("arbitrary", "arbitrary")),
        name="ffn",
    )(h2, x1, mod3, w_up, w_up, cw, cb, wd)


def _block_diag_pairs(w):
    nh, hd, _ = w.shape
    w = w.reshape(nh // 2, 2, hd, hd)
    z = jnp.zeros_like(w[:, 0])
    top = jnp.concatenate([w[:, 0], z], axis=2)
    bot = jnp.concatenate([z, w[:, 1]], axis=2)
    return jnp.concatenate([top, bot], axis=1)


def _cumsum_matrix():
    j = jnp.arange(ATT_TK)[:, None]
    s = jnp.arange(ATT_TK)[None, :]
    return -(j >= s).astype(BF16)


def kernel(x, c, w_ada, b_ada, g_norm1, w_in, conv_rnn_w, conv_rnn_b, w_rg_a, b_rg_a, w_rg_x,
           b_rg_x, lru_lambda, g_q, g_k, w_proj_rnn, w_proj_sb, w_out, g_norm2, w_up,
           conv_ffn_w, conv_ffn_b, w_down):
    bsz, seq, d = x.shape
    depth = w_ada.shape[0]
    rnn_w = conv_rnn_w.shape[2]
    dh = g_q.shape[1]
    sb_w = w_proj_sb.shape[1]
    dff = conv_ffn_w.shape[2]
    gate_col = rnn_w // LANES
    q_off = 2 * rnn_w
    att_blk = ATT_HEADS * dh
    q_col = q_off // att_blk
    k_col = (q_off + sb_w) // att_blk
    v_col = (q_off + 2 * sb_w) // att_blk
    gm_col = (q_off + 3 * sb_w) // (d // 2)
    assert dh == LANES and q_off % att_blk == 0 and sb_w % att_blk == 0
    assert (q_off + 3 * sb_w) % (d // 2) == 0
    uu = _cumsum_matrix()

    x2 = x.reshape(bsz * seq, d)
    for l in range(depth):
        mod3 = _ada(c, w_ada[l], b_ada[l]).reshape(bsz, 6, d)
        p = _inproj(x2, mod3, g_norm1[l].reshape(1, d), w_in[l].astype(BF16), seq)

        wg = (0.5 * jnp.concatenate([_block_diag_pairs(w_rg_a[l]), _block_diag_pairs(w_rg_x[l])],
                                    axis=2)).astype(BF16)
        ya = _rglru(p, conv_rnn_w[l], conv_rnn_b[l].reshape(1, rnn_w), wg,
                    b_rg_a[l].reshape(1, rnn_w), b_rg_x[l].reshape(1, rnn_w),
                    lru_lambda[l].reshape(1, rnn_w), bsz, seq, 0, gate_col)
        ob = _attn(p, g_q[l].reshape(1, dh), g_k[l].reshape(1, dh), uu, bsz, seq,
                   sb_w // dh, dh, q_col, k_col, v_col)
        x1, h2 = _merge(ya, ob, p, x2, mod3, g_norm2[l].reshape(1, d),
                        w_proj_rnn[l].astype(BF16), w_proj_sb[l].astype(BF16),
                        w_out[l].astype(BF16), seq, gm_col)
        x2 = _ffn(h2, x1, mod3, w_up[l].astype(BF16), conv_ffn_w[l],
                  conv_ffn_b[l].reshape(1, dff), w_down[l].astype(BF16), seq)
    return x2.reshape(bsz, seq, d)
```

```python
import functools

import jax
import jax.numpy as jnp
from jax import lax
from jax.experimental import pallas as pl
from jax.experimental.pallas import tpu as pltpu

F32 = jnp.float32
BF16 = jnp.bfloat16

EPS = 1e-6
C_RG = 8.0
N_RNN_HEADS = 16
N_SB_HEADS = 8
RNN_CONV_W = 4
FFN_CONV_W = 3

LANES = 128
SUBLANES = 8
VMEM_LIMIT = 56 * 1024 * 1024

SCAN_FAN = 8
ATT_TQ = 512
ATT_TK = 256
ATT_HEADS = 4
LOG2_E = 1.4426950408889634


def _params(sem, vmem=VMEM_LIMIT, flags=None):
    return pltpu.CompilerParams(dimension_semantics=sem, vmem_limit_bytes=vmem, flags=flags)


def _ada_kernel(c_ref, w_ref, b_ref, o_ref):
    bsz = c_ref.shape[0]
    a = jax.nn.silu(c_ref[...])
    a_hi = a.astype(BF16)
    a_lo = (a - a_hi.astype(F32)).astype(BF16)
    w = w_ref[...]
    w_hi = w.astype(BF16)
    w_lo = (w - w_hi.astype(F32)).astype(BF16)
    r = jnp.dot(jnp.concatenate([a_hi, a_lo], axis=0), w_hi, preferred_element_type=F32)
    o_ref[...] = (r[:bsz] + r[bsz:] + jnp.dot(a_hi, w_lo, preferred_element_type=F32)
                  + b_ref[...])


def _ada(c, w, b, tn=1024):
    bsz, d = c.shape
    n = w.shape[1]
    return pl.pallas_call(
        _ada_kernel,
        grid=(n // tn,),
        in_specs=[pl.BlockSpec((bsz, d), lambda j: (0, 0)),
                  pl.BlockSpec((d, tn), lambda j: (0, j)),
                  pl.BlockSpec((1, tn), lambda j: (0, j))],
        out_specs=pl.BlockSpec((bsz, tn), lambda j: (0, j)),
        out_shape=jax.ShapeDtypeStruct((bsz, n), F32),
        compiler_params=_params(("arbitrary",)),
        name="ada",
    )(c, w, b.reshape(1, n))


def _norm_rows(x_ref, h_ref, g, scale, shift, r0, rows, chunk=128):
    for c0 in range(r0, r0 + rows, chunk):
        x = x_ref[c0:c0 + chunk, :]
        ms = jnp.mean(x * x, axis=-1, keepdims=True)
        y = x * lax.rsqrt(ms + EPS) * g
        h_ref[c0:c0 + chunk, :] = (y * (1.0 + scale) + shift).astype(h_ref.dtype)


def _inproj_kernel(x_ref, mod_ref, g_ref, w_ref, o_ref, h_s, *, rows):
    tm = x_ref.shape[0]

    def project(r0, nrows):
        o_ref[r0:r0 + nrows, :] = jnp.dot(h_s[r0:r0 + nrows, :], w_ref[...],
                                          preferred_element_type=F32).astype(o_ref.dtype)

    @pl.when(pl.program_id(1) == 0)
    def _():
        for r0 in range(0, tm, rows):
            _norm_rows(x_ref, h_s, g_ref[...], mod_ref[0, 1:2, :], mod_ref[0, 0:1, :], r0, rows)
            project(r0, rows)

    @pl.when(pl.program_id(1) != 0)
    def _():
        project(0, tm)


def _inproj(x2, mod3, g1, w, seq, tm=1024, tn=1536, rows=256):
    t, d = x2.shape
    n = w.shape[1]
    tpb = seq // tm
    return pl.pallas_call(
        functools.partial(_inproj_kernel, rows=rows),
        grid=(t // tm, n // tn),
        in_specs=[pl.BlockSpec((tm, d), lambda i, j: (i, 0)),
                  pl.BlockSpec((1,) + mod3.shape[1:], lambda i, j: (i // tpb, 0, 0)),
                  pl.BlockSpec((1, d), lambda i, j: (0, 0)),
                  pl.BlockSpec((d, tn), lambda i, j: (0, j))],
        out_specs=pl.BlockSpec((tm, tn), lambda i, j: (i, j)),
        out_shape=jax.ShapeDtypeStruct((t, n), BF16),
        scratch_shapes=[pltpu.VMEM((tm, d), BF16)],
        compiler_params=_params(("arbitrary", "arbitrary")),
        name="inproj",
    )(x2, mod3, g1, w)


def _scan_up(a_ref, b_ref, n, p_ref, l_ref, a2_ref, b2_ref, grp):
    m = n // SCAN_FAN
    for g0 in range(0, m, grp):
        p = l = None
        for k in range(SCAN_FAN):
            ak = a_ref[pl.ds(g0 * SCAN_FAN + k, grp, stride=SCAN_FAN), :]
            bk = b_ref[pl.ds(g0 * SCAN_FAN + k, grp, stride=SCAN_FAN), :]
            if k == 0:
                p, l = ak, bk
            else:
                l = ak * l + bk
                p = ak * p
            p_ref[k, g0:g0 + grp, :] = p
            l_ref[k, g0:g0 + grp, :] = l
        a2_ref[g0:g0 + grp, :] = p
        b2_ref[g0:g0 + grp, :] = l


def _scan_down(p_ref, l_ref, hin_ref, m, out_ref, grp, inclusive):
    for g0 in range(0, m, grp):
        e = hin_ref[g0:g0 + grp, :]
        for k in range(SCAN_FAN):
            if inclusive:
                v = l_ref[k, g0:g0 + grp, :] + p_ref[k, g0:g0 + grp, :] * e
            elif k == 0:
                v = e
            else:
                v = l_ref[k - 1, g0:g0 + grp, :] + p_ref[k - 1, g0:g0 + grp, :] * e
            out_ref[pl.ds(g0 * SCAN_FAN + k, grp, stride=SCAN_FAN), :] = v


def _rglru_kernel(x_ref, gt_ref, cw_ref, cb_ref, wg_ref, ba_ref, bx_ref, lam_ref, *rest,
                  chunk, n_cast):
    cast_in, rest = rest[:n_cast], rest[n_cast:]
    o_ref, cast_out, scratch = rest[0], rest[1:1 + n_cast], rest[1 + n_cast:]
    xbuf, a_s, b_s, p1, l1, a2, b2, hin1, p2, l2, a3, b3, e3 = scratch
    seq = x_ref.shape[0]
    n2 = seq // SCAN_FAN
    n3 = n2 // SCAN_FAN

    for src, dst in zip(cast_in, cast_out):
        dst[...] = src[...].astype(dst.dtype)

    xbuf[0:SUBLANES, :] = jnp.zeros((SUBLANES, LANES), F32)
    for r0 in range(0, seq, chunk):
        xbuf[SUBLANES + r0:SUBLANES + r0 + chunk, :] = x_ref[r0:r0 + chunk, :].astype(F32)

    lam = lam_ref[...]
    neg = -lam
    softplus_neg_lam = jnp.maximum(neg, 0.0) + jnp.log1p(jnp.exp(-jnp.abs(neg)))
    half_coef = 0.5 * C_RG * softplus_neg_lam
    half_ba = 0.5 * ba_ref[...]
    half_bx = 0.5 * bx_ref[...]

    for r0 in range(0, seq, chunk):
        xr = cb_ref[...]
        for j in range(RNN_CONV_W):
            off = SUBLANES + r0 - (RNN_CONV_W - 1 - j)
            xr = xr + cw_ref[j:j + 1, :] * xbuf[off:off + chunk, :]
        g = jnp.dot(xr.astype(BF16), wg_ref[0], preferred_element_type=F32)
        tr = jnp.tanh(g[:, :LANES] + half_ba)
        i = 0.5 + 0.5 * jnp.tanh(g[:, LANES:] + half_bx)
        y = half_coef + half_coef * tr
        a_s[r0:r0 + chunk, :] = jnp.exp2(y * -LOG2_E)
        t = jnp.tanh(y)
        b_s[r0:r0 + chunk, :] = lax.rsqrt(0.5 + 0.5 / t) * (i * xr)

    _scan_up(a_s, b_s, seq, p1, l1, a2, b2, grp=64)
    _scan_up(a2, b2, n2, p2, l2, a3, b3, grp=n3)
    h = jnp.zeros((1, LANES), F32)
    for r in range(n3):
        e3[r:r + 1, :] = h
        h = a3[r:r + 1, :] * h + b3[r:r + 1, :]
    _scan_down(p2, l2, e3, n3, hin1, grp=n3, inclusive=False)
    _scan_down(p1, l1, hin1, n2, b_s, grp=64, inclusive=True)

    for r0 in range(0, seq, chunk):
        gt = gt_ref[r0:r0 + chunk, :].astype(F32)
        o_ref[r0:r0 + chunk, :] = (jax.nn.gelu(gt) * b_s[r0:r0 + chunk, :]).astype(o_ref.dtype)


def _rglru(p, cw, cb, wg, ba, bx, lam, to_cast, bsz, seq, x_col, gate_col, chunk=256):
    width = cw.shape[1]
    nblk = width // LANES
    n2 = seq // SCAN_FAN
    n3 = n2 // SCAN_FAN
    steps = bsz * nblk
    vec = lambda: pl.BlockSpec((1, LANES), lambda b, c: (0, c))

    def cast_spec(w):
        rows = w.shape[0] // steps
        assert rows * steps == w.shape[0] and rows % (2 * SUBLANES) == 0
        return pl.BlockSpec((rows, w.shape[1]), lambda b, c: (b * nblk + c, 0))

    return pl.pallas_call(
        functools.partial(_rglru_kernel, chunk=chunk, n_cast=len(to_cast)),
        grid=(bsz, nblk),
        in_specs=[pl.BlockSpec((seq, LANES), lambda b, c: (b, x_col + c)),
                  pl.BlockSpec((seq, LANES), lambda b, c: (b, gate_col + c)),
                  pl.BlockSpec((RNN_CONV_W, LANES), lambda b, c: (0, c)),
                  vec(),
                  pl.BlockSpec((1, LANES, 2 * LANES), lambda b, c: (c, 0, 0)),
                  vec(), vec(), vec()] + [cast_spec(w) for w in to_cast],
        out_specs=[pl.BlockSpec((seq, LANES), lambda b, c: (b, c))]
        + [cast_spec(w) for w in to_cast],
        out_shape=[jax.ShapeDtypeStruct((bsz * seq, width), BF16)]
        + [jax.ShapeDtypeStruct(w.shape, BF16) for w in to_cast],
        scratch_shapes=[pltpu.VMEM((seq + SUBLANES, LANES), F32),
                        pltpu.VMEM((seq, LANES), F32), pltpu.VMEM((seq, LANES), F32),
                        pltpu.VMEM((SCAN_FAN, n2, LANES), F32), pltpu.VMEM((SCAN_FAN, n2, LANES), F32),
                        pltpu.VMEM((n2, LANES), F32), pltpu.VMEM((n2, LANES), F32),
                        pltpu.VMEM((n2, LANES), F32),
                        pltpu.VMEM((SCAN_FAN, n3, LANES), F32), pltpu.VMEM((SCAN_FAN, n3, LANES), F32),
                        pltpu.VMEM((n3, LANES), F32), pltpu.VMEM((n3, LANES), F32),
                        pltpu.VMEM((n3, LANES), F32)],
        compiler_params=_params(("arbitrary", "arbitrary")),
        name="rglru",
    )(p, p, cw, cb, wg, ba, bx, lam, *to_cast)


def _attn_kernel(q_ref, k_ref, v_ref, gq_ref, gk_ref, uu_ref, o_ref, qn_s, kn_s, acc_s, run_s):
    seq = q_ref.shape[0]
    dh = gq_ref.shape[1]
    scale = dh ** -0.5 * LOG2_E

    def norm_body(r, _):
        r0 = pl.multiple_of(r * ATT_TQ, ATT_TQ)
        for h in range(ATT_HEADS):
            hs = slice(h * dh, (h + 1) * dh)
            q = q_ref[pl.ds(r0, ATT_TQ), hs].astype(F32)
            qn = q * lax.rsqrt(jnp.mean(q * q, axis=-1, keepdims=True) + EPS) * gq_ref[...]
            qn_s[h, pl.ds(r0, ATT_TQ), :] = (qn * scale).astype(BF16)
            k = k_ref[pl.ds(r0, ATT_TQ), hs].astype(F32)
            kn = k * lax.rsqrt(jnp.mean(k * k, axis=-1, keepdims=True) + EPS) * gk_ref[...]
            kn_s[h, pl.ds(r0, ATT_TQ), :] = kn.astype(BF16)
        return 0
    lax.fori_loop(0, seq // ATT_TQ, norm_body, 0)

    nsub = ATT_TQ // ATT_TK

    heads = range(ATT_HEADS)

    def block(q0, k0, masked):
        zs = [lax.dot_general(qn_s[h, pl.ds(q0, ATT_TQ), :], kn_s[h, pl.ds(k0, ATT_TQ), :],
                              (((1,), (1,)), ((), ())), preferred_element_type=F32)
              for h in heads]
        if masked:
            tri = (lax.broadcasted_iota(jnp.int32, (ATT_TK, ATT_TK), 1)
                   < lax.broadcasted_iota(jnp.int32, (ATT_TK, ATT_TK), 0))

        def on_diag_rows(x, fill):
            top = jnp.where(tri, x[:ATT_TK], fill)
            return top if x.shape[0] == ATT_TK else jnp.concatenate([top, x[ATT_TK:]], axis=0)

        def neg_log_keep(z):
            neg_abs = pltpu.bitcast(pltpu.bitcast(z, jnp.uint32) | jnp.uint32(0x80000000), F32)
            return jnp.maximum(z, 0.0) + jnp.log2(1.0 + jnp.exp2(neg_abs))

        runs = [run_s[h] for h in heads]
        ws = [[None] * nsub for _ in heads]
        if not masked:
            ss = [neg_log_keep(zs[h]) for h in heads]
        for j in reversed(range(nsub)):
            sl = slice(j * ATT_TK, (j + 1) * ATT_TK)
            r0 = j * ATT_TK if masked else 0
            for h in heads:
                z = zs[h][r0:, sl]
                s = on_diag_rows(neg_log_keep(z), 0.0) if masked else ss[h][:, sl]
                cs = jnp.dot(s.astype(BF16), uu_ref[...], preferred_element_type=F32)
                run = runs[h][r0:]
                w = jnp.exp2(z + cs + jnp.concatenate([run] * (ATT_TK // LANES), axis=1))
                if masked:
                    w = on_diag_rows(w, 0.0)
                w = w.astype(BF16)
                run = run - jnp.sum(s, axis=-1, keepdims=True)
                if r0:
                    w = jnp.concatenate([jnp.zeros((r0, ATT_TK), BF16), w], axis=0)
                    run = jnp.concatenate([runs[h][:r0], run], axis=0)
                ws[h][j] = w
                runs[h] = run
        for h in heads:
            run_s[h] = runs[h]
            vblk = v_ref[pl.ds(k0, ATT_TQ), h * dh:(h + 1) * dh]
            acc_s[h] += jnp.dot(jnp.concatenate(ws[h], axis=1), vblk, preferred_element_type=F32)

    def q_body(qi, _):
        q0 = pl.multiple_of(qi * ATT_TQ, ATT_TQ)
        acc_s[...] = jnp.zeros_like(acc_s)
        run_s[...] = jnp.zeros_like(run_s)
        block(q0, q0, True)

        def k_body(j, _):
            block(q0, pl.multiple_of(q0 - (j + 1) * ATT_TQ, ATT_TQ), False)
            return 0
        lax.fori_loop(0, qi, k_body, 0)
        for h in range(ATT_HEADS):
            o_ref[pl.ds(q0, ATT_TQ), h * dh:(h + 1) * dh] = acc_s[h].astype(o_ref.dtype)
        return 0
    lax.fori_loop(0, seq // ATT_TQ, q_body, 0)


def _attn(p, gq, gk, uu, bsz, seq, n_heads, dh, q_col, k_col, v_col):
    wblk = ATT_HEADS * dh
    return pl.pallas_call(
        _attn_kernel,
        grid=(bsz, n_heads // ATT_HEADS),
        in_specs=[pl.BlockSpec((seq, wblk), lambda b, h: (b, q_col + h)),
                  pl.BlockSpec((seq, wblk), lambda b, h: (b, k_col + h)),
                  pl.BlockSpec((seq, wblk), lambda b, h: (b, v_col + h)),
                  pl.BlockSpec((1, dh), lambda b, h: (0, 0)),
                  pl.BlockSpec((1, dh), lambda b, h: (0, 0)),
                  pl.BlockSpec(uu.shape, lambda b, h: (0, 0))],
        out_specs=pl.BlockSpec((seq, wblk), lambda b, h: (b, h)),
        out_shape=jax.ShapeDtypeStruct((bsz * seq, n_heads * dh), BF16),
        scratch_shapes=[pltpu.VMEM((ATT_HEADS, seq, dh), BF16),
                        pltpu.VMEM((ATT_HEADS, seq, dh), BF16),
                        pltpu.VMEM((ATT_HEADS, ATT_TQ, dh), F32),
                        pltpu.VMEM((ATT_HEADS, ATT_TQ, LANES), F32)],
        compiler_params=_params(("arbitrary", "arbitrary")),
        name="attn",
    )(p, p, p, gq, gk, uu)


def _merge_kernel(ya_ref, ob_ref, gr0_ref, gr1_ref, gs0_ref, gs1_ref, x_ref, mod_ref, g2_ref,
                  wr_ref, wsb_ref, wo_ref, x1_ref, h2_ref, mg_s, *, nc, rows):
    tm, d = x_ref.shape
    half = d // 2
    for n0 in range(0, d, nc):
        yr = jnp.dot(ya_ref[...], wr_ref[:, n0:n0 + nc], preferred_element_type=F32)
        ys = jnp.dot(ob_ref[...], wsb_ref[:, n0:n0 + nc], preferred_element_type=F32)
        gr_ref, gs_ref = (gr0_ref, gs0_ref) if n0 < half else (gr1_ref, gs1_ref)
        m0 = n0 % half
        gr = jax.nn.sigmoid(gr_ref[:, m0:m0 + nc].astype(F32))
        gs = jax.nn.sigmoid(gs_ref[:, m0:m0 + nc].astype(F32))
        mg_s[:, n0:n0 + nc] = (gr * yr + gs * ys).astype(BF16)
    for r0 in range(0, tm, rows):
        rs = slice(r0, r0 + rows)
        for n0 in range(0, d, nc):
            y = jnp.dot(mg_s[rs, :], wo_ref[:, n0:n0 + nc], preferred_element_type=F32)
            x1_ref[rs, n0:n0 + nc] = x_ref[rs, n0:n0 + nc] + mod_ref[0, 2:3, n0:n0 + nc] * y
        _norm_rows(x1_ref, h2_ref, g2_ref[...], mod_ref[0, 4:5, :], mod_ref[0, 3:4, :], r0, rows)


def _merge(ya, ob, p, x2, mod3, g2, wr, wsb, wo, seq, gm_col, tm=512, nc=512, rows=256):
    t, d = x2.shape
    w = ya.shape[1]
    half = d // 2
    tpb = seq // tm
    const = lambda shape: pl.BlockSpec(shape, lambda i: (0, 0), pipeline_mode=pl.Buffered(1))
    gate = lambda k: pl.BlockSpec((tm, half), lambda i: (i, gm_col + k))
    return pl.pallas_call(
        functools.partial(_merge_kernel, nc=nc, rows=rows),
        grid=(t // tm,),
        in_specs=[pl.BlockSpec((tm, w), lambda i: (i, 0)),
                  pl.BlockSpec((tm, w), lambda i: (i, 0)),
                  gate(0), gate(1), gate(2), gate(3),
                  pl.BlockSpec((tm, d), lambda i: (i, 0)),
                  pl.BlockSpec((1,) + mod3.shape[1:], lambda i: (i // tpb, 0, 0)),
                  pl.BlockSpec((1, d), lambda i: (0, 0)),
                  const(wr.shape), const(wsb.shape), const(wo.shape)],
        out_specs=[pl.BlockSpec((tm, d), lambda i: (i, 0)),
                   pl.BlockSpec((tm, d), lambda i: (i, 0))],
        out_shape=[jax.ShapeDtypeStruct((t, d), F32), jax.ShapeDtypeStruct((t, d), BF16)],
        scratch_shapes=[pltpu.VMEM((tm, d), BF16)],
        compiler_params=_params(("arbitrary",)),
        name="merge",
    )(ya, ob, p, p, p, p, x2, mod3, g2, wr, wsb, wo)


def _ffn_kernel(h_ref, x1_ref, mod_ref, wv_ref, wg_ref, cw_ref, cb_ref, wd_ref, o_ref,
                ubuf, halo, act_s, val_s, *, tiles_per_seq, sub, nsub):
    i = pl.program_id(0)
    f = pl.program_id(1)
    tm = h_ref.shape[0]
    tf = wv_ref.shape[1]
    d = o_ref.shape[1]

    @pl.when(f == 0)
    def _():
        o_ref[...] = jnp.zeros_like(o_ref)

    @pl.when(i % tiles_per_seq == 0)
    def _():
        ubuf[0:SUBLANES, :] = jnp.zeros((SUBLANES, tf), F32)

    @pl.when(i % tiles_per_seq != 0)
    def _():
        ubuf[0:SUBLANES, :] = halo[f]

    h = h_ref[...]

    def up(c0):
        cs = slice(c0, c0 + sub)
        val_s[:, cs] = jnp.dot(h, wv_ref[:, cs], preferred_element_type=F32)
        ubuf[SUBLANES:, cs] = jnp.dot(h, wg_ref[:, cs], preferred_element_type=F32)

    def gate(c0):
        cs = slice(c0, c0 + sub)
        g = cb_ref[:, cs]
        for j in range(FFN_CONV_W):
            off = SUBLANES - (FFN_CONV_W - 1 - j)
            g = g + cw_ref[j:j + 1, cs] * ubuf[off:off + tm, cs]
        act_s[:, cs] = (jax.nn.gelu(g) * val_s[:, cs]).astype(BF16)

    def down(k0):
        ks = slice(k0, k0 + tf // 2)
        for n0 in range(0, d, nsub):
            o_ref[:, n0:n0 + nsub] += jnp.dot(act_s[:, ks], wd_ref[ks, n0:n0 + nsub],
                                              preferred_element_type=F32)

    up(0)
    for c0 in range(sub, tf, sub):
        up(c0)
        gate(c0 - sub)
    assert (tf // 2) % sub == 0 and tf // 2 <= tf - sub
    down(0)
    gate(tf - sub)
    halo[f] = ubuf[tm:tm + SUBLANES, :]
    down(tf // 2)

    @pl.when(f == pl.num_programs(1) - 1)
    def _():
        o_ref[...] = x1_ref[...] + mod_ref[0, 5:6, :] * o_ref[...]


def _ffn(h2, x1, mod3, w_up, cw, cb, wd, seq, tm=512, tf=1024, sub=256, nsub=512):
    t, d = x1.shape
    dff = w_up.shape[1] // 2
    tpb = seq // tm
    nf = dff // tf
    return pl.pallas_call(
        functools.partial(_ffn_kernel, tiles_per_seq=tpb, sub=sub, nsub=nsub),
        grid=(t // tm, nf),
        in_specs=[pl.BlockSpec((tm, d), lambda i, f: (i, 0)),
                  pl.BlockSpec((tm, d), lambda i, f: (i, 0)),
                  pl.BlockSpec((1,) + mod3.shape[1:], lambda i, f: (i // tpb, 0, 0)),
                  pl.BlockSpec((d, tf), lambda i, f: (0, f)),
                  pl.BlockSpec((d, tf), lambda i, f: (0, nf + f)),
                  pl.BlockSpec((FFN_CONV_W, tf), lambda i, f: (0, f)),
                  pl.BlockSpec((1, tf), lambda i, f: (0, f)),
                  pl.BlockSpec((tf, d), lambda i, f: (f, 0))],
        out_specs=pl.BlockSpec((tm, d), lambda i, f: (i, 0)),
        out_shape=jax.ShapeDtypeStruct((t, d), F32),
        scratch_shapes=[pltpu.VMEM((tm + SUBLANES, tf), F32),
                        pltpu.VMEM((nf, SUBLANES, tf), F32),
                        pltpu.VMEM((tm, tf), BF16),
                        pltpu.VMEM((tm, tf), F32)],
        compiler_params=_params(("arbitrary", "arbitrary")),
        name="ffn",
    )(h2, x1, mod3, w_up, w_up, cw, cb, wd)


def _block_diag_pairs(w):
    nh, hd, _ = w.shape
    w = w.reshape(nh // 2, 2, hd, hd)
    z = jnp.zeros_like(w[:, 0])
    top = jnp.concatenate([w[:, 0], z], axis=2)
    bot = jnp.concatenate([z, w[:, 1]], axis=2)
    return jnp.concatenate([top, bot], axis=1)


def _cumsum_matrix():
    j = jnp.arange(ATT_TK)[:, None]
    s = jnp.arange(ATT_TK)[None, :]
    return -(j >= s).astype(BF16)


def kernel(x, c, w_ada, b_ada, g_norm1, w_in, conv_rnn_w, conv_rnn_b, w_rg_a, b_rg_a, w_rg_x,
           b_rg_x, lru_lambda, g_q, g_k, w_proj_rnn, w_proj_sb, w_out, g_norm2, w_up,
           conv_ffn_w, conv_ffn_b, w_down):
    bsz, seq, d = x.shape
    depth = w_ada.shape[0]
    rnn_w = conv_rnn_w.shape[2]
    dh = g_q.shape[1]
    sb_w = w_proj_sb.shape[1]
    dff = conv_ffn_w.shape[2]
    gate_col = rnn_w // LANES
    q_off = 2 * rnn_w
    att_blk = ATT_HEADS * dh
    q_col = q_off // att_blk
    k_col = (q_off + sb_w) // att_blk
    v_col = (q_off + 2 * sb_w) // att_blk
    gm_col = (q_off + 3 * sb_w) // (d // 2)
    assert dh == LANES and q_off % att_blk == 0 and sb_w % att_blk == 0
    assert (q_off + 3 * sb_w) % (d // 2) == 0
    uu = _cumsum_matrix()

    x2 = x.reshape(bsz * seq, d)
    for l in range(depth):
        mod3 = _ada(c, w_ada[l], b_ada[l]).reshape(bsz, 6, d)
        p = _inproj(x2, mod3, g_norm1[l].reshape(1, d), w_in[l].astype(BF16), seq)

        wg = (0.5 * jnp.concatenate([_block_diag_pairs(w_rg_a[l]), _block_diag_pairs(w_rg_x[l])],
                                    axis=2)).astype(BF16)
        ya, w_pr, w_ps, w_o, w_u, w_d = _rglru(
            p, conv_rnn_w[l], conv_rnn_b[l].reshape(1, rnn_w), wg,
            b_rg_a[l].reshape(1, rnn_w), b_rg_x[l].reshape(1, rnn_w),
            lru_lambda[l].reshape(1, rnn_w),
            (w_proj_rnn[l], w_proj_sb[l], w_out[l], w_up[l], w_down[l]), bsz, seq, 0, gate_col)
        ob = _attn(p, g_q[l].reshape(1, dh), g_k[l].reshape(1, dh), uu, bsz, seq,
                   sb_w // dh, dh, q_col, k_col, v_col)
        x1, h2 = _merge(ya, ob, p, x2, mod3, g_norm2[l].reshape(1, d), w_pr, w_ps, w_o,
                        seq, gm_col)
        x2 = _ffn(h2, x1, mod3, w_u, conv_ffn_w[l], conv_ffn_b[l].reshape(1, dff), w_d, seq)
    return x2.reshape(bsz, seq, d)
```

```python
import functools

import jax
import jax.numpy as jnp
from jax import lax
from jax.experimental import pallas as pl
from jax.experimental.pallas import tpu as pltpu

F32 = jnp.float32
BF16 = jnp.bfloat16

EPS = 1e-6
C_RG = 8.0
N_RNN_HEADS = 16
N_SB_HEADS = 8
RNN_CONV_W = 4
FFN_CONV_W = 3

LANES = 128
SUBLANES = 8
VMEM_LIMIT = 56 * 1024 * 1024

SCAN_FAN = 8
ATT_TQ = 512
ATT_TK = 256
ATT_HEADS = 4
LOG2_E = 1.4426950408889634


def _params(sem, vmem=VMEM_LIMIT, flags=None):
    return pltpu.CompilerParams(dimension_semantics=sem, vmem_limit_bytes=vmem, flags=flags)


def _ada_kernel(c_ref, w_ref, b_ref, o_ref):
    bsz = c_ref.shape[0]
    a = jax.nn.silu(c_ref[...])
    a_hi = a.astype(BF16)
    a_lo = (a - a_hi.astype(F32)).astype(BF16)
    w = w_ref[...]
    w_hi = w.astype(BF16)
    w_lo = (w - w_hi.astype(F32)).astype(BF16)
    r = jnp.dot(jnp.concatenate([a_hi, a_lo], axis=0), w_hi, preferred_element_type=F32)
    o_ref[...] = (r[:bsz] + r[bsz:] + jnp.dot(a_hi, w_lo, preferred_element_type=F32)
                  + b_ref[...])


def _ada(c, w, b, tn=1024):
    bsz, d = c.shape
    n = w.shape[1]
    return pl.pallas_call(
        _ada_kernel,
        grid=(n // tn,),
        in_specs=[pl.BlockSpec((bsz, d), lambda j: (0, 0)),
                  pl.BlockSpec((d, tn), lambda j: (0, j)),
                  pl.BlockSpec((1, tn), lambda j: (0, j))],
        out_specs=pl.BlockSpec((bsz, tn), lambda j: (0, j)),
        out_shape=jax.ShapeDtypeStruct((bsz, n), F32),
        compiler_params=_params(("arbitrary",)),
        name="ada",
    )(c, w, b.reshape(1, n))


def _norm_rows(x_ref, h_ref, g, scale, shift, r0, rows, chunk=128):
    for c0 in range(r0, r0 + rows, chunk):
        x = x_ref[c0:c0 + chunk, :]
        ms = jnp.mean(x * x, axis=-1, keepdims=True)
        y = x * lax.rsqrt(ms + EPS) * g
        h_ref[c0:c0 + chunk, :] = (y * (1.0 + scale) + shift).astype(h_ref.dtype)


def _inproj_kernel(x_ref, mod_ref, g_ref, w_ref, o_ref, h_s, *, rows):
    tm = x_ref.shape[0]

    def project(r0, nrows):
        o_ref[r0:r0 + nrows, :] = jnp.dot(h_s[r0:r0 + nrows, :], w_ref[...],
                                          preferred_element_type=F32).astype(o_ref.dtype)

    @pl.when(pl.program_id(1) == 0)
    def _():
        for r0 in range(0, tm, rows):
            _norm_rows(x_ref, h_s, g_ref[...], mod_ref[0, 1:2, :], mod_ref[0, 0:1, :], r0, rows)
            project(r0, rows)

    @pl.when(pl.program_id(1) != 0)
    def _():
        project(0, tm)


def _inproj(x2, mod3, g1, w, seq, tm=1024, tn=2304, rows=256):
    t, d = x2.shape
    n = w.shape[1]
    tpb = seq // tm
    return pl.pallas_call(
        functools.partial(_inproj_kernel, rows=rows),
        grid=(t // tm, n // tn),
        in_specs=[pl.BlockSpec((tm, d), lambda i, j: (i, 0)),
                  pl.BlockSpec((1,) + mod3.shape[1:], lambda i, j: (i // tpb, 0, 0)),
                  pl.BlockSpec((1, d), lambda i, j: (0, 0)),
                  pl.BlockSpec((d, tn), lambda i, j: (0, j))],
        out_specs=pl.BlockSpec((tm, tn), lambda i, j: (i, j)),
        out_shape=jax.ShapeDtypeStruct((t, n), BF16),
        scratch_shapes=[pltpu.VMEM((tm, d), BF16)],
        compiler_params=_params(("arbitrary", "arbitrary")),
        name="inproj",
    )(x2, mod3, g1, w)


def _scan_up(a_ref, b_ref, n, p_ref, l_ref, a2_ref, b2_ref, grp):
    m = n // SCAN_FAN
    for g0 in range(0, m, grp):
        p = l = None
        for k in range(SCAN_FAN):
            ak = a_ref[pl.ds(g0 * SCAN_FAN + k, grp, stride=SCAN_FAN), :]
            bk = b_ref[pl.ds(g0 * SCAN_FAN + k, grp, stride=SCAN_FAN), :]
            if k == 0:
                p, l = ak, bk
            else:
                l = ak * l + bk
                p = ak * p
            p_ref[k, g0:g0 + grp, :] = p
            l_ref[k, g0:g0 + grp, :] = l
        a2_ref[g0:g0 + grp, :] = p
        b2_ref[g0:g0 + grp, :] = l


def _scan_down(p_ref, l_ref, hin_ref, m, out_ref, grp, inclusive):
    for g0 in range(0, m, grp):
        e = hin_ref[g0:g0 + grp, :]
        for k in range(SCAN_FAN):
            if inclusive:
                v = l_ref[k, g0:g0 + grp, :] + p_ref[k, g0:g0 + grp, :] * e
            elif k == 0:
                v = e
            else:
                v = l_ref[k - 1, g0:g0 + grp, :] + p_ref[k - 1, g0:g0 + grp, :] * e
            out_ref[pl.ds(g0 * SCAN_FAN + k, grp, stride=SCAN_FAN), :] = v


def _rglru_kernel(x_ref, gt_ref, cw_ref, cb_ref, wg_ref, ba_ref, bx_ref, lam_ref, *rest,
                  chunk, n_cast):
    cast_in, rest = rest[:n_cast], rest[n_cast:]
    o_ref, cast_out, scratch = rest[0], rest[1:1 + n_cast], rest[1 + n_cast:]
    xbuf, a_s, b_s, p1, l1, a2, b2, hin1, p2, l2, a3, b3, e3 = scratch
    seq = x_ref.shape[0]
    n2 = seq // SCAN_FAN
    n3 = n2 // SCAN_FAN

    for src, dst in zip(cast_in, cast_out):
        dst[...] = src[...].astype(dst.dtype)

    xbuf[0:SUBLANES, :] = jnp.zeros((SUBLANES, LANES), F32)
    for r0 in range(0, seq, chunk):
        xbuf[SUBLANES + r0:SUBLANES + r0 + chunk, :] = x_ref[r0:r0 + chunk, :].astype(F32)

    lam = lam_ref[...]
    neg = -lam
    softplus_neg_lam = jnp.maximum(neg, 0.0) + jnp.log1p(jnp.exp(-jnp.abs(neg)))
    half_coef = 0.5 * C_RG * softplus_neg_lam
    half_ba = 0.5 * ba_ref[...]
    half_bx = 0.5 * bx_ref[...]

    for r0 in range(0, seq, chunk):
        xr = cb_ref[...]
        for j in range(RNN_CONV_W):
            off = SUBLANES + r0 - (RNN_CONV_W - 1 - j)
            xr = xr + cw_ref[j:j + 1, :] * xbuf[off:off + chunk, :]
        g = jnp.dot(xr.astype(BF16), wg_ref[0], preferred_element_type=F32)
        tr = jnp.tanh(g[:, :LANES] + half_ba)
        i = 0.5 + 0.5 * jnp.tanh(g[:, LANES:] + half_bx)
        y = half_coef + half_coef * tr
        a_s[r0:r0 + chunk, :] = jnp.exp2(y * -LOG2_E)
        t = jnp.tanh(y)
        b_s[r0:r0 + chunk, :] = lax.rsqrt(0.5 + 0.5 / t) * (i * xr)

    _scan_up(a_s, b_s, seq, p1, l1, a2, b2, grp=64)
    _scan_up(a2, b2, n2, p2, l2, a3, b3, grp=n3)
    h = jnp.zeros((1, LANES), F32)
    for r in range(n3):
        e3[r:r + 1, :] = h
        h = a3[r:r + 1, :] * h + b3[r:r + 1, :]
    _scan_down(p2, l2, e3, n3, hin1, grp=n3, inclusive=False)
    _scan_down(p1, l1, hin1, n2, b_s, grp=64, inclusive=True)

    for r0 in range(0, seq, chunk):
        gt = gt_ref[r0:r0 + chunk, :].astype(F32)
        o_ref[r0:r0 + chunk, :] = (jax.nn.gelu(gt) * b_s[r0:r0 + chunk, :]).astype(o_ref.dtype)


def _rglru(p, cw, cb, wg, ba, bx, lam, to_cast, bsz, seq, x_col, gate_col, chunk=256):
    width = cw.shape[1]
    nblk = width // LANES
    n2 = seq // SCAN_FAN
    n3 = n2 // SCAN_FAN
    steps = bsz * nblk
    vec = lambda: pl.BlockSpec((1, LANES), lambda b, c: (0, c))

    def cast_spec(w):
        rows = w.shape[0] // steps
        assert rows * steps == w.shape[0] and rows % (2 * SUBLANES) == 0
        return pl.BlockSpec((rows, w.shape[1]), lambda b, c: (b * nblk + c, 0))

    return pl.pallas_call(
        functools.partial(_rglru_kernel, chunk=chunk, n_cast=len(to_cast)),
        grid=(bsz, nblk),
        in_specs=[pl.BlockSpec((seq, LANES), lambda b, c: (b, x_col + c)),
                  pl.BlockSpec((seq, LANES), lambda b, c: (b, gate_col + c)),
                  pl.BlockSpec((RNN_CONV_W, LANES), lambda b, c: (0, c)),
                  vec(),
                  pl.BlockSpec((1, LANES, 2 * LANES), lambda b, c: (c, 0, 0)),
                  vec(), vec(), vec()] + [cast_spec(w) for w in to_cast],
        out_specs=[pl.BlockSpec((seq, LANES), lambda b, c: (b, c))]
        + [cast_spec(w) for w in to_cast],
        out_shape=[jax.ShapeDtypeStruct((bsz * seq, width), BF16)]
        + [jax.ShapeDtypeStruct(w.shape, BF16) for w in to_cast],
        scratch_shapes=[pltpu.VMEM((seq + SUBLANES, LANES), F32),
                        pltpu.VMEM((seq, LANES), F32), pltpu.VMEM((seq, LANES), F32),
                        pltpu.VMEM((SCAN_FAN, n2, LANES), F32), pltpu.VMEM((SCAN_FAN, n2, LANES), F32),
                        pltpu.VMEM((n2, LANES), F32), pltpu.VMEM((n2, LANES), F32),
                        pltpu.VMEM((n2, LANES), F32),
                        pltpu.VMEM((SCAN_FAN, n3, LANES), F32), pltpu.VMEM((SCAN_FAN, n3, LANES), F32),
                        pltpu.VMEM((n3, LANES), F32), pltpu.VMEM((n3, LANES), F32),
                        pltpu.VMEM((n3, LANES), F32)],
        compiler_params=_params(("arbitrary", "arbitrary")),
        name="rglru",
    )(p, p, cw, cb, wg, ba, bx, lam, *to_cast)


def _attn_kernel(q_ref, k_ref, v_ref, gq_ref, gk_ref, uu_ref, o_ref, qn_s, kn_s, acc_s, run_s):
    seq = q_ref.shape[0]
    dh = gq_ref.shape[1]
    scale = dh ** -0.5 * LOG2_E

    def norm_body(r, _):
        r0 = pl.multiple_of(r * ATT_TQ, ATT_TQ)
        for h in range(ATT_HEADS):
            hs = slice(h * dh, (h + 1) * dh)
            q = q_ref[pl.ds(r0, ATT_TQ), hs].astype(F32)
            qn = q * lax.rsqrt(jnp.mean(q * q, axis=-1, keepdims=True) + EPS) * gq_ref[...]
            qn_s[h, pl.ds(r0, ATT_TQ), :] = (qn * scale).astype(BF16)
            k = k_ref[pl.ds(r0, ATT_TQ), hs].astype(F32)
            kn = k * lax.rsqrt(jnp.mean(k * k, axis=-1, keepdims=True) + EPS) * gk_ref[...]
            kn_s[h, pl.ds(r0, ATT_TQ), :] = kn.astype(BF16)
        return 0
    lax.fori_loop(0, seq // ATT_TQ, norm_body, 0)

    nsub = ATT_TQ // ATT_TK

    heads = range(ATT_HEADS)

    def block(q0, k0, masked):
        zs = [lax.dot_general(qn_s[h, pl.ds(q0, ATT_TQ), :], kn_s[h, pl.ds(k0, ATT_TQ), :],
                              (((1,), (1,)), ((), ())), preferred_element_type=F32)
              for h in heads]
        if masked:
            tri = (lax.broadcasted_iota(jnp.int32, (ATT_TK, ATT_TK), 1)
                   < lax.broadcasted_iota(jnp.int32, (ATT_TK, ATT_TK), 0))

        def on_diag_rows(x, fill):
            top = jnp.where(tri, x[:ATT_TK], fill)
            return top if x.shape[0] == ATT_TK else jnp.concatenate([top, x[ATT_TK:]], axis=0)

        def neg_log_keep(z):
            neg_abs = pltpu.bitcast(pltpu.bitcast(z, jnp.uint32) | jnp.uint32(0x80000000), F32)
            return jnp.maximum(z, 0.0) + jnp.log2(1.0 + jnp.exp2(neg_abs))

        runs = [run_s[h] for h in heads]
        ws = [[None] * nsub for _ in heads]
        if not masked:
            ss = [neg_log_keep(zs[h]) for h in heads]
        for j in reversed(range(nsub)):
            sl = slice(j * ATT_TK, (j + 1) * ATT_TK)
            r0 = j * ATT_TK if masked else 0
            for h in heads:
                z = zs[h][r0:, sl]
                s = on_diag_rows(neg_log_keep(z), 0.0) if masked else ss[h][:, sl]
                cs = jnp.dot(s.astype(BF16), uu_ref[...], preferred_element_type=F32)
                run = runs[h][r0:]
                w = jnp.exp2(z + cs + jnp.concatenate([run] * (ATT_TK // LANES), axis=1))
                if masked:
                    w = on_diag_rows(w, 0.0)
                w = w.astype(BF16)
                run = run - jnp.sum(s, axis=-1, keepdims=True)
                if r0:
                    w = jnp.concatenate([jnp.zeros((r0, ATT_TK), BF16), w], axis=0)
                    run = jnp.concatenate([runs[h][:r0], run], axis=0)
                ws[h][j] = w
                runs[h] = run
        for h in heads:
            run_s[h] = runs[h]
            vblk = v_ref[pl.ds(k0, ATT_TQ), h * dh:(h + 1) * dh]
            acc_s[h] += jnp.dot(jnp.concatenate(ws[h], axis=1), vblk, preferred_element_type=F32)

    def q_body(qi, _):
        q0 = pl.multiple_of(qi * ATT_TQ, ATT_TQ)
        acc_s[...] = jnp.zeros_like(acc_s)
        run_s[...] = jnp.zeros_like(run_s)
        block(q0, q0, True)

        def k_body(j, _):
            block(q0, pl.multiple_of(q0 - (j + 1) * ATT_TQ, ATT_TQ), False)
            return 0
        lax.fori_loop(0, qi, k_body, 0)
        for h in range(ATT_HEADS):
            o_ref[pl.ds(q0, ATT_TQ), h * dh:(h + 1) * dh] = acc_s[h].astype(o_ref.dtype)
        return 0
    lax.fori_loop(0, seq // ATT_TQ, q_body, 0)


def _attn(p, gq, gk, uu, bsz, seq, n_heads, dh, q_col, k_col, v_col):
    wblk = ATT_HEADS * dh
    return pl.pallas_call(
        _attn_kernel,
        grid=(bsz, n_heads // ATT_HEADS),
        in_specs=[pl.BlockSpec((seq, wblk), lambda b, h: (b, q_col + h)),
                  pl.BlockSpec((seq, wblk), lambda b, h: (b, k_col + h)),
                  pl.BlockSpec((seq, wblk), lambda b, h: (b, v_col + h)),
                  pl.BlockSpec((1, dh), lambda b, h: (0, 0)),
                  pl.BlockSpec((1, dh), lambda b, h: (0, 0)),
                  pl.BlockSpec(uu.shape, lambda b, h: (0, 0))],
        out_specs=pl.BlockSpec((seq, wblk), lambda b, h: (b, h)),
        out_shape=jax.ShapeDtypeStruct((bsz * seq, n_heads * dh), BF16),
        scratch_shapes=[pltpu.VMEM((ATT_HEADS, seq, dh), BF16),
                        pltpu.VMEM((ATT_HEADS, seq, dh), BF16),
                        pltpu.VMEM((ATT_HEADS, ATT_TQ, dh), F32),
                        pltpu.VMEM((ATT_HEADS, ATT_TQ, LANES), F32)],
        compiler_params=_params(("arbitrary", "arbitrary")),
        name="attn",
    )(p, p, p, gq, gk, uu)


def _merge_kernel(ya_ref, ob_ref, gr0_ref, gr1_ref, gs0_ref, gs1_ref, x_ref, mod_ref, g2_ref,
                  wr_ref, wsb_ref, wo_ref, x1_ref, h2_ref, mg_s, *, nc, rows):
    tm, d = x_ref.shape
    half = d // 2
    for n0 in range(0, d, nc):
        yr = jnp.dot(ya_ref[...], wr_ref[:, n0:n0 + nc], preferred_element_type=F32)
        ys = jnp.dot(ob_ref[...], wsb_ref[:, n0:n0 + nc], preferred_element_type=F32)
        gr_ref, gs_ref = (gr0_ref, gs0_ref) if n0 < half else (gr1_ref, gs1_ref)
        m0 = n0 % half
        gr = jax.nn.sigmoid(gr_ref[:, m0:m0 + nc].astype(F32))
        gs = jax.nn.sigmoid(gs_ref[:, m0:m0 + nc].astype(F32))
        mg_s[:, n0:n0 + nc] = (gr * yr + gs * ys).astype(BF16)
    for r0 in range(0, tm, rows):
        rs = slice(r0, r0 + rows)
        for n0 in range(0, d, nc):
            y = jnp.dot(mg_s[rs, :], wo_ref[:, n0:n0 + nc], preferred_element_type=F32)
            x1_ref[rs, n0:n0 + nc] = x_ref[rs, n0:n0 + nc] + mod_ref[0, 2:3, n0:n0 + nc] * y
        _norm_rows(x1_ref, h2_ref, g2_ref[...], mod_ref[0, 4:5, :], mod_ref[0, 3:4, :], r0, rows)


def _merge(ya, ob, p, x2, mod3, g2, wr, wsb, wo, seq, gm_col, tm=512, nc=512, rows=256):
    t, d = x2.shape
    w = ya.shape[1]
    half = d // 2
    tpb = seq // tm
    const = lambda shape: pl.BlockSpec(shape, lambda i: (0, 0), pipeline_mode=pl.Buffered(1))
    gate = lambda k: pl.BlockSpec((tm, half), lambda i: (i, gm_col + k))
    return pl.pallas_call(
        functools.partial(_merge_kernel, nc=nc, rows=rows),
        grid=(t // tm,),
        in_specs=[pl.BlockSpec((tm, w), lambda i: (i, 0)),
                  pl.BlockSpec((tm, w), lambda i: (i, 0)),
                  gate(0), gate(1), gate(2), gate(3),
                  pl.BlockSpec((tm, d), lambda i: (i, 0)),
                  pl.BlockSpec((1,) + mod3.shape[1:], lambda i: (i // tpb, 0, 0)),
                  pl.BlockSpec((1, d), lambda i: (0, 0)),
                  const(wr.shape), const(wsb.shape), const(wo.shape)],
        out_specs=[pl.BlockSpec((tm, d), lambda i: (i, 0)),
                   pl.BlockSpec((tm, d), lambda i: (i, 0))],
        out_shape=[jax.ShapeDtypeStruct((t, d), F32), jax.ShapeDtypeStruct((t, d), BF16)],
        scratch_shapes=[pltpu.VMEM((tm, d), BF16)],
        compiler_params=_params(("arbitrary",)),
        name="merge",
    )(ya, ob, p, p, p, p, x2, mod3, g2, wr, wsb, wo)


def _ffn_kernel(h_ref, x1_ref, mod_ref, wv_ref, wg_ref, cw_ref, cb_ref, wd_ref, o_ref,
                ubuf, halo, act_s, val_s, *, tiles_per_seq, sub, nsub):
    i = pl.program_id(0)
    f = pl.program_id(1)
    tm = h_ref.shape[0]
    tf = wv_ref.shape[1]
    d = o_ref.shape[1]

    @pl.when(f == 0)
    def _():
        o_ref[...] = jnp.zeros_like(o_ref)

    @pl.when(i % tiles_per_seq == 0)
    def _():
        ubuf[0:SUBLANES, :] = jnp.zeros((SUBLANES, tf), F32)

    @pl.when(i % tiles_per_seq != 0)
    def _():
        ubuf[0:SUBLANES, :] = halo[f]

    h = h_ref[...]

    def up(c0):
        cs = slice(c0, c0 + sub)
        val_s[:, cs] = jnp.dot(h, wv_ref[:, cs], preferred_element_type=F32)
        ubuf[SUBLANES:, cs] = jnp.dot(h, wg_ref[:, cs], preferred_element_type=F32)

    def gate(c0):
        cs = slice(c0, c0 + sub)
        g = cb_ref[:, cs]
        for j in range(FFN_CONV_W):
            off = SUBLANES - (FFN_CONV_W - 1 - j)
            g = g + cw_ref[j:j + 1, cs] * ubuf[off:off + tm, cs]
        act_s[:, cs] = (jax.nn.gelu(g) * val_s[:, cs]).astype(BF16)

    def down(k0):
        ks = slice(k0, k0 + tf // 2)
        for n0 in range(0, d, nsub):
            o_ref[:, n0:n0 + nsub] += jnp.dot(act_s[:, ks], wd_ref[ks, n0:n0 + nsub],
                                              preferred_element_type=F32)

    up(0)
    for c0 in range(sub, tf, sub):
        up(c0)
        gate(c0 - sub)
    assert (tf // 2) % sub == 0 and tf // 2 <= tf - sub
    down(0)
    gate(tf - sub)
    halo[f] = ubuf[tm:tm + SUBLANES, :]
    down(tf // 2)

    @pl.when(f == pl.num_programs(1) - 1)
    def _():
        o_ref[...] = x1_ref[...] + mod_ref[0, 5:6, :] * o_ref[...]


def _ffn(h2, x1, mod3, w_up, cw, cb, wd, seq, tm=512, tf=1024, sub=256, nsub=512):
    t, d = x1.shape
    dff = w_up.shape[1] // 2
    tpb = seq // tm
    nf = dff // tf
    return pl.pallas_call(
        functools.partial(_ffn_kernel, tiles_per_seq=tpb, sub=sub, nsub=nsub),
        grid=(t // tm, nf),
        in_specs=[pl.BlockSpec((tm, d), lambda i, f: (i, 0)),
                  pl.BlockSpec((tm, d), lambda i, f: (i, 0)),
                  pl.BlockSpec((1,) + mod3.shape[1:], lambda i, f: (i // tpb, 0, 0)),
                  pl.BlockSpec((d, tf), lambda i, f: (0, f)),
                  pl.BlockSpec((d, tf), lambda i, f: (0, nf + f)),
                  pl.BlockSpec((FFN_CONV_W, tf), lambda i, f: (0, f)),
                  pl.BlockSpec((1, tf), lambda i, f: (0, f)),
                  pl.BlockSpec((tf, d), lambda i, f: (f, 0))],
        out_specs=pl.BlockSpec((tm, d), lambda i, f: (i, 0)),
        out_shape=jax.ShapeDtypeStruct((t, d), F32),
        scratch_shapes=[pltpu.VMEM((tm + SUBLANES, tf), F32),
                        pltpu.VMEM((nf, SUBLANES, tf), F32),
                        pltpu.VMEM((tm, tf), BF16),
                        pltpu.VMEM((tm, tf), F32)],
        compiler_params=_params(("arbitrary", "arbitrary")),
        name="ffn",
    )(h2, x1, mod3, w_up, w_up, cw, cb, wd)


def _block_diag_pairs(w):
    nh, hd, _ = w.shape
    w = w.reshape(nh // 2, 2, hd, hd)
    z = jnp.zeros_like(w[:, 0])
    top = jnp.concatenate([w[:, 0], z], axis=2)
    bot = jnp.concatenate([z, w[:, 1]], axis=2)
    return jnp.concatenate([top, bot], axis=1)


def _cumsum_matrix():
    j = jnp.arange(ATT_TK)[:, None]
    s = jnp.arange(ATT_TK)[None, :]
    return -(j >= s).astype(BF16)


def kernel(x, c, w_ada, b_ada, g_norm1, w_in, conv_rnn_w, conv_rnn_b, w_rg_a, b_rg_a, w_rg_x,
           b_rg_x, lru_lambda, g_q, g_k, w_proj_rnn, w_proj_sb, w_out, g_norm2, w_up,
           conv_ffn_w, conv_ffn_b, w_down):
    bsz, seq, d = x.shape
    depth = w_ada.shape[0]
    rnn_w = conv_rnn_w.shape[2]
    dh = g_q.shape[1]
    sb_w = w_proj_sb.shape[1]
    dff = conv_ffn_w.shape[2]
    gate_col = rnn_w // LANES
    q_off = 2 * rnn_w
    att_blk = ATT_HEADS * dh
    q_col = q_off // att_blk
    k_col = (q_off + sb_w) // att_blk
    v_col = (q_off + 2 * sb_w) // att_blk
    gm_col = (q_off + 3 * sb_w) // (d // 2)
    assert dh == LANES and q_off % att_blk == 0 and sb_w % att_blk == 0
    assert (q_off + 3 * sb_w) % (d // 2) == 0
    uu = _cumsum_matrix()

    x2 = x.reshape(bsz * seq, d)
    for l in range(depth):
        mod3 = _ada(c, w_ada[l], b_ada[l]).reshape(bsz, 6, d)
        p = _inproj(x2, mod3, g_norm1[l].reshape(1, d), w_in[l].astype(BF16), seq)

        wg = (0.5 * jnp.concatenate([_block_diag_pairs(w_rg_a[l]), _block_diag_pairs(w_rg_x[l])],
                                    axis=2)).astype(BF16)
        ya, w_pr, w_ps, w_o, w_u, w_d = _rglru(
            p, conv_rnn_w[l], conv_rnn_b[l].reshape(1, rnn_w), wg,
            b_rg_a[l].reshape(1, rnn_w), b_rg_x[l].reshape(1, rnn_w),
            lru_lambda[l].reshape(1, rnn_w),
            (w_proj_rnn[l], w_proj_sb[l], w_out[l], w_up[l], w_down[l]), bsz, seq, 0, gate_col)
        ob = _attn(p, g_q[l].reshape(1, dh), g_k[l].reshape(1, dh), uu, bsz, seq,
                   sb_w // dh, dh, q_col, k_col, v_col)
        x1, h2 = _merge(ya, ob, p, x2, mod3, g_norm2[l].reshape(1, d), w_pr, w_ps, w_o,
                        seq, gm_col)
        x2 = _ffn(h2, x1, mod3, w_u, conv_ffn_w[l], conv_ffn_b[l].reshape(1, dff), w_d, seq)
    return x2.reshape(bsz, seq, d)
```
